```python
import math
import jax
import jax.numpy as jnp
from jax import lax
import numpy as np

D_MODEL = 1024
BATCH = 32
SEQ = 2048
DEPTH = 4
DEC_BATCH = 16
DEC_SEQ = 32
PAST_LEN = 1024

CHUNK = 64
S5_WIDTH = D_MODEL // 2
S5_GROUP_CH = 16
S5_GROUPS = S5_WIDTH // S5_GROUP_CH
S5_STATE = 64
SB_WIDTH = D_MODEL - S5_WIDTH
SB_HEAD_DIM = 64
SB_HEADS = SB_WIDTH // SB_HEAD_DIM
SB_BLOCK = 128
N_EXPERTS = 32
TOP_K = 4
D_FF = D_MODEL
SWIGLU_LIMIT = 7.0
SWIGLU_ALPHA = 1.702
MOE_BLOCK = 128
LN_EPS = 1e-5
DEEPNORM_ALPHA = (2 * DEPTH) ** 0.25
DEEPNORM_BETA = (8 * DEPTH) ** -0.25

kernel_name = 'hybrid_s5_stickbreak_moe_stream_step'


def layer_norm(x, g, b):
    xf = x.astype(jnp.float32)
    mu = jnp.mean(xf, axis=-1, keepdims=True)
    var = jnp.mean(jnp.square(xf - mu), axis=-1, keepdims=True)
    return ((xf - mu) * lax.rsqrt(var + LN_EPS) * g.astype(jnp.float32) + b.astype(jnp.float32)).astype(x.dtype)


def adaln_params(c, w_ada, b_ada):
    mod = jax.nn.silu(c) @ w_ada + b_ada
    return [m[:, None, :] for m in jnp.split(mod, 6, axis=-1)]


def _ssm_combine(e1, e2):
    a1r, a1i, b1r, b1i = e1
    a2r, a2i, b2r, b2i = e2
    return (a2r * a1r - a2i * a1i,
            a2r * a1i + a2i * a1r,
            a2r * b1r - a2i * b1i + b2r,
            a2r * b1i + a2i * b1r + b2i)


def s5_mixer(u, a_re, a_im, log_dt, b_re, b_im, c_re, c_im, d, w_glu, b_glu, x0_re, x0_im):
    f32 = jnp.float32
    bsz, L, _ = u.shape
    a_re, a_im = a_re.astype(f32), a_im.astype(f32)
    dt = jnp.exp(log_dt.astype(f32))[:, None]
    mag = jnp.exp(a_re * dt)
    ab_re = mag * jnp.cos(a_im * dt)
    ab_im = mag * jnp.sin(a_im * dt)
    den = a_re * a_re + a_im * a_im
    nr, ni = ab_re - 1.0, ab_im
    f_re = (nr * a_re + ni * a_im) / den
    f_im = (ni * a_re - nr * a_im) / den
    b_re, b_im = b_re.astype(f32), b_im.astype(f32)
    bb_re = f_re[..., None] * b_re - f_im[..., None] * b_im
    bb_im = f_re[..., None] * b_im + f_im[..., None] * b_re
    uf = u.astype(f32)
    ug = uf.reshape(bsz, L, S5_GROUPS, S5_GROUP_CH)
    bu_re = jnp.einsum('blgn,gpn->blgp', ug, bb_re)
    bu_im = jnp.einsum('blgn,gpn->blgp', ug, bb_im)
    shape_a = (1, L, S5_GROUPS, S5_STATE)
    cum_re, cum_im, xs_re, xs_im = lax.associative_scan(
        _ssm_combine,
        (jnp.broadcast_to(ab_re, shape_a), jnp.broadcast_to(ab_im, shape_a), bu_re, bu_im),
        axis=1)
    if x0_re is not None:
        x0r = x0_re.astype(f32)[:, None]
        x0i = x0_im.astype(f32)[:, None]
        xs_re = xs_re + cum_re * x0r - cum_im * x0i
        xs_im = xs_im + cum_re * x0i + cum_im * x0r
    y = (jnp.einsum('gnp,blgp->blgn', c_re.astype(f32), xs_re)
         - jnp.einsum('gnp,blgp->blgn', c_im.astype(f32), xs_im))
    y = y.reshape(bsz, L, S5_WIDTH) + d.astype(f32) * uf
    g = jax.nn.gelu(y)
    out = g * jax.nn.sigmoid(g @ w_glu.astype(f32) + b_glu.astype(f32))
    return out.astype(u.dtype), xs_re[:, -1].astype(u.dtype), xs_im[:, -1].astype(u.dtype)


def _stick_breaking_block(q, k, v, q_pos):
    z = jnp.einsum('bhqd,bhkd->bhqk', q.astype(jnp.float32), k.astype(jnp.float32)) * (SB_HEAD_DIM ** -0.5)
    k_pos = jnp.arange(k.shape[2])
    mask = k_pos[None, :] < q_pos[:, None]
    log_not = jnp.where(mask, jax.nn.log_sigmoid(-z), 0.0)
    after = lax.cumsum(log_not, axis=3, reverse=True) - log_not
    w = jnp.where(mask, jnp.exp(jax.nn.log_sigmoid(z) + after), 0.0)
    return jnp.einsum('bhqk,bhkd->bhqd', w, v.astype(jnp.float32)).astype(q.dtype)


def stick_breaking_prompt(q, k, v):
    L = q.shape[2]
    outs = []
    for qs in range(0, L, SB_BLOCK):
        qe = min(qs + SB_BLOCK, L)
        outs.append(_stick_breaking_block(q[:, :, qs:qe], k[:, :, :qe], v[:, :, :qe], jnp.arange(qs, qe)))
    return jnp.concatenate(outs, axis=2)


def stick_breaking_sample(q, k, v, k_past, v_past):
    past = k_past.shape[2]
    k_all = jnp.concatenate([k_past.astype(k.dtype), k], axis=2)
    v_all = jnp.concatenate([v_past.astype(v.dtype), v], axis=2)
    return _stick_breaking_block(q, k_all, v_all, past + jnp.arange(q.shape[2]))


def moe_ffn(h, w_router, b_router, w_gate_up, b_gate_up, w_down, b_down):
    T, D = h.shape
    logits = (h @ w_router + b_router).astype(jnp.float32)
    top_val, top_idx = lax.top_k(logits, TOP_K)
    gates = jax.nn.softmax(top_val, axis=-1).astype(h.dtype)
    n_assign = T * TOP_K
    e_flat = top_idx.reshape(-1)
    tok_flat = jnp.repeat(jnp.arange(T, dtype=jnp.int32), TOP_K)
    g_flat = gates.reshape(-1)
    order = jnp.argsort(e_flat)
    e_sorted = e_flat[order]
    counts = jnp.bincount(e_flat, length=N_EXPERTS)
    padded = (counts + MOE_BLOCK - 1) // MOE_BLOCK * MOE_BLOCK
    start = jnp.cumsum(counts) - counts
    pend = jnp.cumsum(padded)
    pstart = pend - padded
    dest = pstart[e_sorted] + (jnp.arange(n_assign) - start[e_sorted])
    n_blocks = (n_assign + N_EXPERTS * (MOE_BLOCK - 1) + MOE_BLOCK - 1) // MOE_BLOCK
    n_rows = n_blocks * MOE_BLOCK
    row_tok = jnp.full((n_rows,), T, jnp.int32).at[dest].set(tok_flat[order])
    row_gate = jnp.zeros((n_rows,), h.dtype).at[dest].set(g_flat[order])
    block_expert = jnp.minimum(
        jnp.searchsorted(pend, jnp.arange(n_blocks) * MOE_BLOCK, side='right'), N_EXPERTS - 1)
    h_pad = jnp.concatenate([h, jnp.zeros((1, D), h.dtype)], axis=0)

    def run_block(args):
        tok, e = args
        gu = h_pad[tok] @ w_gate_up[e] + b_gate_up[e]
        gate = jnp.minimum(gu[:, :D_FF], SWIGLU_LIMIT)
        up = jnp.clip(gu[:, D_FF:], -SWIGLU_LIMIT, SWIGLU_LIMIT)
        act = (up + 1.0) * gate * jax.nn.sigmoid(SWIGLU_ALPHA * gate)
        return act @ w_down[e] + b_down[e]

    out_rows = lax.map(run_block, (row_tok.reshape(n_blocks, MOE_BLOCK), block_expert))
    out_rows = out_rows.reshape(n_rows, D) * row_gate[:, None]
    return jnp.zeros((T + 1, D), h.dtype).at[row_tok].add(out_rows)[:T]


def setup_inputs(seed: int = 0) -> dict:
    key = jax.random.key(seed)
    ks = jax.random.split(key, 40)
    f32 = jnp.float32

    def nrm(k, shape, s):
        return jax.random.normal(k, shape, f32) * s

    n_idx = jnp.arange(S5_STATE, dtype=f32)
    return {
        'x_prompt': nrm(ks[0], (BATCH, SEQ, D_MODEL), 1.0),
        'x_sample': nrm(ks[1], (DEC_BATCH, DEC_SEQ, D_MODEL), 1.0),
        'cache_k': nrm(ks[2], (DEPTH, DEC_BATCH, SB_HEADS, PAST_LEN, SB_HEAD_DIM), 1.0),
        'cache_v': nrm(ks[3], (DEPTH, DEC_BATCH, SB_HEADS, PAST_LEN, SB_HEAD_DIM), 1.0),
        'state_s5_re': nrm(ks[4], (DEPTH, DEC_BATCH, S5_GROUPS, S5_STATE), 0.1),
        'state_s5_im': nrm(ks[5], (DEPTH, DEC_BATCH, S5_GROUPS, S5_STATE), 0.1),
        'c_prompt': nrm(ks[6], (BATCH, D_MODEL), 1.0),
        'c_sample': nrm(ks[7], (DEC_BATCH, D_MODEL), 1.0),
        'ln_in_g': 1.0 + nrm(ks[8], (D_MODEL,), 0.02),
        'ln_in_b': nrm(ks[9], (D_MODEL,), 0.02),
        'w_ada': nrm(ks[10], (DEPTH, D_MODEL, 6 * D_MODEL), 0.1 * D_MODEL ** -0.5),
        'b_ada': nrm(ks[11], (DEPTH, 6 * D_MODEL), 0.02),
        'w_in': nrm(ks[12], (DEPTH, D_MODEL, S5_WIDTH + 3 * SB_WIDTH), D_MODEL ** -0.5),
        's5_a_re': -0.5 + nrm(ks[13], (DEPTH, S5_GROUPS, S5_STATE), 0.01),
        's5_a_im': jnp.broadcast_to(math.pi * n_idx, (DEPTH, S5_GROUPS, S5_STATE)).astype(f32),
        's5_log_dt': jax.random.uniform(ks[14], (DEPTH, S5_GROUPS), f32, math.log(1e-3), math.log(1e-1)),
        's5_b_re': nrm(ks[15], (DEPTH, S5_GROUPS, S5_STATE, S5_GROUP_CH), (2 * S5_GROUP_CH) ** -0.5),
        's5_b_im': nrm(ks[16], (DEPTH, S5_GROUPS, S5_STATE, S5_GROUP_CH), (2 * S5_GROUP_CH) ** -0.5),
        's5_c_re': nrm(ks[17], (DEPTH, S5_GROUPS, S5_GROUP_CH, S5_STATE), S5_STATE ** -0.5),
        's5_c_im': nrm(ks[18], (DEPTH, S5_GROUPS, S5_GROUP_CH, S5_STATE), S5_STATE ** -0.5),
        's5_d': nrm(ks[19], (DEPTH, S5_WIDTH), 1.0),
        's5_w_glu': nrm(ks[20], (DEPTH, S5_WIDTH, S5_WIDTH), S5_WIDTH ** -0.5),
        's5_b_glu': nrm(ks[21], (DEPTH, S5_WIDTH), 0.02),
        'w_out': nrm(ks[22], (DEPTH, D_MODEL, D_MODEL), DEEPNORM_BETA * D_MODEL ** -0.5),
        'ln1_g': 1.0 + nrm(ks[23], (DEPTH, D_MODEL), 0.02),
        'ln1_b': nrm(ks[24], (DEPTH, D_MODEL), 0.02),
        'w_router': nrm(ks[25], (DEPTH, D_MODEL, N_EXPERTS), D_MODEL ** -0.5),
        'b_router': nrm(ks[26], (DEPTH, N_EXPERTS), 0.01),
        'w_gate_up': nrm(ks[27], (DEPTH, N_EXPERTS, D_MODEL, 2 * D_FF), D_MODEL ** -0.5),
        'b_gate_up': nrm(ks[28], (DEPTH, N_EXPERTS, 2 * D_FF), 0.02),
        'w_down': nrm(ks[29], (DEPTH, N_EXPERTS, D_FF, D_MODEL), DEEPNORM_BETA * D_FF ** -0.5),
        'b_down': nrm(ks[30], (DEPTH, N_EXPERTS, D_MODEL), 0.02),
        'ln2_g': 1.0 + nrm(ks[31], (DEPTH, D_MODEL), 0.02),
        'ln2_b': nrm(ks[32], (DEPTH, D_MODEL), 0.02),
    }


def reference(x_prompt, x_sample, cache_k, cache_v, state_s5_re, state_s5_im, c_prompt, c_sample,
              ln_in_g, ln_in_b, w_ada, b_ada, w_in, s5_a_re, s5_a_im, s5_log_dt, s5_b_re, s5_b_im,
              s5_c_re, s5_c_im, s5_d, s5_w_glu, s5_b_glu, w_out, ln1_g, ln1_b, w_router, b_router,
              w_gate_up, b_gate_up, w_down, b_down, ln2_g, ln2_b):
    assert x_sample.shape[1] <= CHUNK

    def run_layer(l, x, c, k_past, v_past, s0_re, s0_im):
        bsz, L, _ = x.shape
        sh1, sc1, g1, sh2, sc2, g2 = adaln_params(c, w_ada[l], b_ada[l])
        h = x * (1.0 + sc1) + sh1
        proj = h @ w_in[l]
        u = proj[..., :S5_WIDTH]
        qkv = proj[..., S5_WIDTH:].reshape(bsz, L, 3, SB_HEADS, SB_HEAD_DIM)
        q, k, v = [qkv[:, :, i].transpose(0, 2, 1, 3) for i in range(3)]
        s5_out, s_re, s_im = s5_mixer(u, s5_a_re[l], s5_a_im[l], s5_log_dt[l], s5_b_re[l], s5_b_im[l],
                                      s5_c_re[l], s5_c_im[l], s5_d[l], s5_w_glu[l], s5_b_glu[l],
                                      s0_re, s0_im)
        if k_past is None:
            attn = stick_breaking_prompt(q, k, v)
        else:
            attn = stick_breaking_sample(q, k, v, k_past, v_past)
        attn = attn.transpose(0, 2, 1, 3).reshape(bsz, L, SB_WIDTH)
        mix = jnp.concatenate([s5_out, attn], axis=-1) @ w_out[l]
        x = layer_norm(DEEPNORM_ALPHA * x + (1.0 + g1) * mix, ln1_g[l], ln1_b[l])
        h2 = x * (1.0 + sc2) + sh2
        ff = moe_ffn(h2.reshape(bsz * L, D_MODEL), w_router[l], b_router[l], w_gate_up[l], b_gate_up[l],
                     w_down[l], b_down[l]).reshape(bsz, L, D_MODEL)
        x = layer_norm(DEEPNORM_ALPHA * x + (1.0 + g2) * ff, ln2_g[l], ln2_b[l])
        return x, k, v, s_re, s_im

    xp = layer_norm(x_prompt, ln_in_g, ln_in_b)
    xs = layer_norm(x_sample, ln_in_g, ln_in_b)
    kp, vp, srp, sip = [], [], [], []
    ksm, vsm, srs, sis = [], [], [], []
    for l in range(DEPTH):
        xp, k, v, s_re, s_im = run_layer(l, xp, c_prompt, None, None, None, None)
        kp.append(k); vp.append(v); srp.append(s_re); sip.append(s_im)
        xs, k, v, s_re, s_im = run_layer(l, xs, c_sample, cache_k[l], cache_v[l], state_s5_re[l], state_s5_im[l])
        ksm.append(k); vsm.append(v); srs.append(s_re); sis.append(s_im)
    return (xp, xs, jnp.stack(kp), jnp.stack(vp), jnp.stack(srp), jnp.stack(sip),
            jnp.stack(ksm), jnp.stack(vsm), jnp.stack(srs), jnp.stack(sis))
```

```python
import functools
import math

import jax
import jax.numpy as jnp
from jax import lax
from jax.experimental import pallas as pl
from jax.experimental.pallas import tpu as pltpu

F32 = jnp.float32
BF16 = jnp.bfloat16
HIGHEST = lax.Precision.HIGHEST

D_MODEL = 1024
DEPTH = 4
S5_WIDTH = D_MODEL // 2
S5_GROUP_CH = 16
S5_GROUPS = S5_WIDTH // S5_GROUP_CH
S5_STATE = 64
S5_NSTATE = S5_GROUPS * S5_STATE
SB_WIDTH = D_MODEL - S5_WIDTH
SB_HEAD_DIM = 64
SB_HEADS = SB_WIDTH // SB_HEAD_DIM
N_EXPERTS = 32
TOP_K = 4
D_FF = D_MODEL
SWIGLU_LIMIT = 7.0
SWIGLU_ALPHA = 1.702
LN_EPS = 1e-5
DEEPNORM_ALPHA = (2 * DEPTH) ** 0.25

LANES = 128
SUBLANES = 8
ROW_TILES = D_MODEL // LANES
VMEM_LIMIT = 56 * 1024 * 1024


def _cparams(n_axes):
    return pltpu.CompilerParams(dimension_semantics=("arbitrary",) * n_axes,
                                vmem_limit_bytes=VMEM_LIMIT)


def _layer_norm(x, g, b):
    mu = jnp.mean(x, axis=-1, keepdims=True)
    xc = x - mu
    var = jnp.mean(xc * xc, axis=-1, keepdims=True)
    return xc * lax.rsqrt(var + LN_EPS) * g + b


def _ada_kernel(c_ref, w_ref, b_ref, o_ref):
    c = c_ref[...]
    s = c * jax.nn.sigmoid(c)
    o_ref[0] = jnp.dot(s, w_ref[0], precision=HIGHEST, preferred_element_type=F32) + b_ref[0]


def adaln_all(c_all, w_ada, b_ada):
    bc = c_all.shape[0]
    return pl.pallas_call(
        _ada_kernel,
        grid=(DEPTH, 6),
        in_specs=[pl.BlockSpec((bc, D_MODEL), lambda l, j: (0, 0)),
                  pl.BlockSpec((1, D_MODEL, D_MODEL), lambda l, j: (l, 0, j)),
                  pl.BlockSpec((1, 1, D_MODEL), lambda l, j: (l, 0, j))],
        out_specs=pl.BlockSpec((1, bc, D_MODEL), lambda l, j: (l, 0, j)),
        out_shape=jax.ShapeDtypeStruct((DEPTH, bc, 6 * D_MODEL), F32),
        compiler_params=_cparams(2),
        name="adaln",
    )(c_all, w_ada, b_ada.reshape(DEPTH, 1, 6 * D_MODEL))


def _ln_kernel(x_ref, g_ref, b_ref, o_ref):
    o_ref[...] = _layer_norm(x_ref[...], g_ref[...], b_ref[...])


def ln_rows(x2d, g, b, tm):
    t = x2d.shape[0]
    return pl.pallas_call(
        _ln_kernel,
        grid=(t // tm,),
        in_specs=[pl.BlockSpec((tm, D_MODEL), lambda i: (i, 0)),
                  pl.BlockSpec((1, D_MODEL), lambda i: (0, 0)),
                  pl.BlockSpec((1, D_MODEL), lambda i: (0, 0))],
        out_specs=pl.BlockSpec((tm, D_MODEL), lambda i: (i, 0)),
        out_shape=jax.ShapeDtypeStruct((t, D_MODEL), F32),
        compiler_params=_cparams(1),
        name="ln_in",
    )(x2d, g.reshape(1, D_MODEL), b.reshape(1, D_MODEL))


def _inproj_kernel(l_ref, x_ref, sc_ref, sh_ref, w_ref, u_ref, q_ref, k_ref, v_ref):
    del l_ref
    h = x_ref[0] * (1.0 + sc_ref[0]) + sh_ref[0]
    proj = jnp.dot(h.astype(BF16), w_ref[0], preferred_element_type=F32)
    u_ref[0] = proj[:, :S5_WIDTH]
    for hh in range(SB_HEADS):
        lo = S5_WIDTH + hh * SB_HEAD_DIM
        q_ref[0, hh] = proj[:, lo:lo + SB_HEAD_DIM]
        k_ref[0, hh] = proj[:, lo + SB_WIDTH:lo + SB_WIDTH + SB_HEAD_DIM]
        v_ref[0, hh] = proj[:, lo + 2 * SB_WIDTH:lo + 2 * SB_WIDTH + SB_HEAD_DIM]


def in_projection(l, x, sc1, sh1, w_in_bf16, tm):
    b, L, _ = x.shape
    n_proj = S5_WIDTH + 3 * SB_WIDTH
    mod_spec = pl.BlockSpec((1, 1, D_MODEL), lambda i, j, l_ref: (i, 0, 0))
    head_spec = pl.BlockSpec((1, SB_HEADS, tm, SB_HEAD_DIM), lambda i, j, l_ref: (i, 0, j, 0))
    head_shape = jax.ShapeDtypeStruct((b, SB_HEADS, L, SB_HEAD_DIM), F32)
    return pl.pallas_call(
        _inproj_kernel,
        grid_spec=pltpu.PrefetchScalarGridSpec(
            num_scalar_prefetch=1,
            grid=(b, L // tm),
            in_specs=[pl.BlockSpec((1, tm, D_MODEL), lambda i, j, l_ref: (i, j, 0)),
                      mod_spec, mod_spec,
                      pl.BlockSpec((1, D_MODEL, n_proj), lambda i, j, l_ref: (l_ref[0], 0, 0))],
            out_specs=[pl.BlockSpec((1, tm, S5_WIDTH), lambda i, j, l_ref: (i, j, 0)),
                       head_spec, head_spec, head_spec]),
        out_shape=[jax.ShapeDtypeStruct((b, L, S5_WIDTH), F32), head_shape, head_shape, head_shape],
        compiler_params=_cparams(2),
        name="in_proj",
    )(l, x, sc1, sh1, w_in_bf16)


S5_BATCH = SUBLANES
S5_SLABS = S5_NSTATE // LANES
S5_CHUNKS = S5_WIDTH // LANES
S5_CHUNK_STATES = S5_NSTATE // S5_CHUNKS
S5_CHUNK_SLABS = S5_CHUNK_STATES // LANES


def _s5_kernel(l_ref, u_ref, x0r_ref, x0i_ref, wb_ref, ar_ref, ai_ref, wc_ref, d_ref, wg_ref, bg_ref,
               o_ref, sr_ref, si_ref, upad, slab, st_re, st_im, *, lc, pitch):
    del l_ref
    ci = pl.program_id(1)

    @pl.when(ci == 0)
    def _():
        upad[...] = jnp.zeros_like(upad)
        st_re[...] = x0r_ref[...]
        st_im[...] = x0i_ref[...]

    for b in range(S5_BATCH):
        upad[b * pitch:b * pitch + lc, :] = u_ref[b]

    for c in range(S5_CHUNKS):
        r = jnp.dot(upad[:, c * LANES:(c + 1) * LANES].astype(BF16), wb_ref[c],
                    preferred_element_type=F32)
        for j in range(S5_CHUNK_SLABS):
            slab[c * S5_CHUNK_SLABS + j] = r[:, j * LANES:(j + 1) * LANES]
            slab[S5_SLABS + c * S5_CHUNK_SLABS + j] = r[:, S5_CHUNK_STATES + j * LANES:
                                                          S5_CHUNK_STATES + (j + 1) * LANES]

    for c in range(S5_CHUNKS):
        lane0 = c * S5_CHUNK_STATES
        ar = [jnp.broadcast_to(ar_ref[:, lane0 + j * LANES:lane0 + (j + 1) * LANES], (S5_BATCH, LANES))
              for j in range(S5_CHUNK_SLABS)]
        ai = [jnp.broadcast_to(ai_ref[:, lane0 + j * LANES:lane0 + (j + 1) * LANES], (S5_BATCH, LANES))
              for j in range(S5_CHUNK_SLABS)]
        xr0 = tuple(st_re[:, lane0 + j * LANES:lane0 + (j + 1) * LANES] for j in range(S5_CHUNK_SLABS))
        xi0 = tuple(st_im[:, lane0 + j * LANES:lane0 + (j + 1) * LANES] for j in range(S5_CHUNK_SLABS))

        def step(t, carry, c=c, ar=ar, ai=ai):
            xr, xi = carry
            nxr, nxi = [], []
            for j in range(S5_CHUNK_SLABS):
                s_re = c * S5_CHUNK_SLABS + j
                s_im = S5_SLABS + s_re
                rows_t = pl.ds(t, S5_BATCH, stride=pitch)
                nr = ar[j] * xr[j] - ai[j] * xi[j] + slab[s_re, rows_t, :]
                ni = ar[j] * xi[j] + ai[j] * xr[j] + slab[s_im, rows_t, :]
                slab[s_re, rows_t, :] = nr
                slab[s_im, rows_t, :] = ni
                nxr.append(nr)
                nxi.append(ni)
            return tuple(nxr), tuple(nxi)

        xr, xi = lax.fori_loop(0, lc, step, (xr0, xi0))
        for j in range(S5_CHUNK_SLABS):
            st_re[:, lane0 + j * LANES:lane0 + (j + 1) * LANES] = xr[j]
            st_im[:, lane0 + j * LANES:lane0 + (j + 1) * LANES] = xi[j]

    sr_ref[...] = st_re[...]
    si_ref[...] = st_im[...]

    ys = []
    for c in range(S5_CHUNKS):
        xs = jnp.concatenate(
            [slab[c * S5_CHUNK_SLABS + j] for j in range(S5_CHUNK_SLABS)]
            + [slab[S5_SLABS + c * S5_CHUNK_SLABS + j] for j in range(S5_CHUNK_SLABS)], axis=1)
        ys.append(jnp.dot(xs.astype(BF16), wc_ref[c], preferred_element_type=F32))
    y = jnp.concatenate(ys, axis=1) + d_ref[...] * upad[...]
    g = jax.nn.gelu(y)
    gate = jnp.dot(g.astype(BF16), wg_ref[0], preferred_element_type=F32) + bg_ref[...]
    out = g * jax.nn.sigmoid(gate)
    for b in range(S5_BATCH):
        o_ref[b] = out[b * pitch:b * pitch + lc, :]


def s5_mixer(l, u, x0_re, x0_im, wb, ab_re, ab_im, wc, d, w_glu_bf16, b_glu, lc):
    b, L, _ = u.shape
    pitch = lc + SUBLANES
    rows = S5_BATCH * pitch
    kern = functools.partial(_s5_kernel, lc=lc, pitch=pitch)
    const2 = lambda i, j, l_ref: (0, 0)
    const3 = lambda i, j, l_ref: (0, 0, 0)
    state_spec = pl.BlockSpec((S5_BATCH, S5_NSTATE), lambda i, j, l_ref: (i, 0))
    return pl.pallas_call(
        kern,
        grid_spec=pltpu.PrefetchScalarGridSpec(
            num_scalar_prefetch=1,
            grid=(b // S5_BATCH, L // lc),
            in_specs=[pl.BlockSpec((S5_BATCH, lc, S5_WIDTH), lambda i, j, l_ref: (i, j, 0)),
                      state_spec, state_spec,
                      pl.BlockSpec((S5_CHUNKS, LANES, 2 * S5_CHUNK_STATES), const3),
                      pl.BlockSpec((1, S5_NSTATE), const2),
                      pl.BlockSpec((1, S5_NSTATE), const2),
                      pl.BlockSpec((S5_CHUNKS, 2 * S5_CHUNK_STATES, LANES), const3),
                      pl.BlockSpec((1, S5_WIDTH), const2),
                      pl.BlockSpec((1, S5_WIDTH, S5_WIDTH), lambda i, j, l_ref: (l_ref[0], 0, 0)),
                      pl.BlockSpec((1, S5_WIDTH), const2)],
            out_specs=[pl.BlockSpec((S5_BATCH, lc, S5_WIDTH), lambda i, j, l_ref: (i, j, 0)),
                       state_spec, state_spec],
            scratch_shapes=[pltpu.VMEM((rows, S5_WIDTH), F32),
                            pltpu.VMEM((2 * S5_SLABS, rows, LANES), F32),
                            pltpu.VMEM((S5_BATCH, S5_NSTATE), F32),
                            pltpu.VMEM((S5_BATCH, S5_NSTATE), F32)]),
        out_shape=[jax.ShapeDtypeStruct((b, L, S5_WIDTH), F32),
                   jax.ShapeDtypeStruct((b, S5_NSTATE), F32),
                   jax.ShapeDtypeStruct((b, S5_NSTATE), F32)],
        compiler_params=_cparams(2),
        name="s5_mixer",
    )(l, u, x0_re, x0_im, wb, ab_re, ab_im, wc, d, w_glu_bf16, b_glu)


def s5_discretise(a_re, a_im, log_dt, b_re, b_im, c_re, c_im):
    a_re, a_im = a_re.astype(F32), a_im.astype(F32)
    dt = jnp.exp(log_dt.astype(F32))[:, None]
    mag = jnp.exp(a_re * dt)
    ab_re = mag * jnp.cos(a_im * dt)
    ab_im = mag * jnp.sin(a_im * dt)
    den = a_re * a_re + a_im * a_im
    nr, ni = ab_re - 1.0, ab_im
    f_re = (nr * a_re + ni * a_im) / den
    f_im = (ni * a_re - nr * a_im) / den
    b_re, b_im = b_re.astype(F32), b_im.astype(F32)
    bb_re = f_re[..., None] * b_re - f_im[..., None] * b_im
    bb_im = f_re[..., None] * b_im + f_im[..., None] * b_re
    gpc = S5_GROUPS // S5_CHUNKS
    eye = jnp.eye(gpc, dtype=F32)

    def blockdiag_in(bb):
        bb = bb.reshape(S5_CHUNKS, gpc, S5_STATE, S5_GROUP_CH)
        m = jnp.einsum('cgpn,gh->cgnhp', bb, eye)
        return m.reshape(S5_CHUNKS, gpc * S5_GROUP_CH, gpc * S5_STATE)

    def blockdiag_out(cc):
        cc = cc.reshape(S5_CHUNKS, gpc, S5_GROUP_CH, S5_STATE)
        m = jnp.einsum('cgnp,gh->cgphn', cc, eye)
        return m.reshape(S5_CHUNKS, gpc * S5_STATE, gpc * S5_GROUP_CH)

    wb = jnp.concatenate([blockdiag_in(bb_re), blockdiag_in(bb_im)], axis=2).astype(BF16)
    wc = jnp.concatenate([blockdiag_out(c_re.astype(F32)), -blockdiag_out(c_im.astype(F32))],
                         axis=1).astype(BF16)
    return ab_re.reshape(1, S5_NSTATE), ab_im.reshape(1, S5_NSTATE), wb, wc


ATTN_HEADS_PER_STEP = LANES // SB_HEAD_DIM


def _attn_kernel(q_ref, k_ref, v_ref, o_ref, *, tq, tk, qoff):
    q0 = pl.program_id(2) * tq
    jd = (qoff + q0) // tk
    scale = SB_HEAD_DIM ** -0.5
    s_idx = lax.broadcasted_iota(jnp.int32, (tk, 2 * tk), 0)
    j_idx = lax.broadcasted_iota(jnp.int32, (tk, 2 * tk), 1)
    suffix_and_total = jnp.where((s_idx > j_idx) | (j_idx >= tk), 1.0, 0.0).astype(BF16)
    q_pos = qoff + q0 + lax.broadcasted_iota(jnp.int32, (tq, tk), 0)
    k_off = lax.broadcasted_iota(jnp.int32, (tq, tk), 1)

    outs = []
    for hh in range(ATTN_HEADS_PER_STEP):
        q = q_ref[0, hh]

        def tile(j, carry, acc, masked, hh=hh, q=q):
            start = pl.multiple_of(j * tk, tk)
            kb = k_ref[0, hh, pl.ds(start, tk), :]
            vb = v_ref[0, hh, pl.ds(start, tk), :]
            z = lax.dot_general(q, kb, (((1,), (1,)), ((), ())), precision=HIGHEST,
                                preferred_element_type=F32) * scale
            sp = jnp.log1p(jnp.exp(-jnp.abs(z)))
            log_not = -jnp.maximum(z, 0.0) - sp
            log_beta = jnp.minimum(z, 0.0) - sp
            if masked:
                mask = (k_off + j * tk) < q_pos
                log_not = jnp.where(mask, log_not, 0.0)
            hi = log_not.astype(BF16)
            lo = (log_not - hi.astype(F32)).astype(BF16)
            cs = (jnp.dot(hi, suffix_and_total, preferred_element_type=F32)
                  + jnp.dot(lo, suffix_and_total, preferred_element_type=F32))
            w = jnp.exp(log_beta + cs[:, :tk] + carry)
            if masked:
                w = jnp.where(mask, w, 0.0)
            acc = acc + jnp.dot(w.astype(BF16), vb.astype(BF16), preferred_element_type=F32)
            return carry + cs[:, tk:], acc

        carry, acc = tile(jd, jnp.zeros((tq, tk), F32), jnp.zeros((tq, SB_HEAD_DIM), F32), True)

        def body(it, ca, tile=tile, jd=jd):
            return tile(jd - 1 - it, ca[0], ca[1], False)

        carry, acc = lax.fori_loop(0, jd, body, (carry, acc))
        outs.append(acc)
    o_ref[0] = jnp.concatenate(outs, axis=1)


def stick_breaking(q, k, v, tq, tk, qoff):
    b, h, L, dh = q.shape
    tkeys = k.shape[2]
    assert tq <= tk and tk % tq == 0 and qoff % tk == 0 and tkeys % tk == 0
    hp = ATTN_HEADS_PER_STEP
    kern = functools.partial(_attn_kernel, tq=tq, tk=tk, qoff=qoff)
    kv_spec = pl.BlockSpec((1, hp, tkeys, dh), lambda i, j, t: (i, j, 0, 0))
    return pl.pallas_call(
        kern,
        grid=(b, h // hp, L // tq),
        in_specs=[pl.BlockSpec((1, hp, tq, dh), lambda i, j, t: (i, j, t, 0)), kv_spec, kv_spec],
        out_specs=pl.BlockSpec((1, tq, hp * dh), lambda i, j, t: (i, t, j)),
        out_shape=jax.ShapeDtypeStruct((b, L, h * dh), F32),
        compiler_params=_cparams(3),
        name="stick_breaking",
    )(q, k, v)


def _lane_pack(cols, tm):
    lane = lax.broadcasted_iota(jnp.int32, (tm, LANES), 1)
    out = jnp.zeros((tm, LANES), cols[0].dtype)
    for kk, col in enumerate(cols):
        out = jnp.where(lane == kk, col, out)
    return out


def _outproj_kernel(l_ref, s5_ref, at_ref, x_ref, g1_ref, sc2_ref, sh2_ref, w_ref, lg_ref, lb_ref,
                    wr_ref, br_ref, x1_ref, h2_ref, idx_ref, gate_ref, rank_ref, cnt_ref, counts, *, tm):
    del l_ref
    first = (pl.program_id(0) == 0) & (pl.program_id(1) == 0)

    @pl.when(first)
    def _():
        counts[...] = jnp.zeros_like(counts)

    mix = (jnp.dot(s5_ref[0].astype(BF16), w_ref[0, :S5_WIDTH, :], preferred_element_type=F32)
           + jnp.dot(at_ref[0].astype(BF16), w_ref[0, S5_WIDTH:, :], preferred_element_type=F32))
    x1 = _layer_norm(DEEPNORM_ALPHA * x_ref[0] + (1.0 + g1_ref[0]) * mix, lg_ref[...], lb_ref[...])
    x1_ref[0] = x1
    h2 = x1 * (1.0 + sc2_ref[0]) + sh2_ref[0]
    for s in range(ROW_TILES):
        h2_ref[pl.ds(s, tm, stride=ROW_TILES), :] = h2[:, s * LANES:(s + 1) * LANES]

    logits = jnp.dot(h2, wr_ref[...], precision=HIGHEST, preferred_element_type=F32) + br_ref[...]
    lane = lax.broadcasted_iota(jnp.int32, (tm, LANES), 1)
    work = logits
    vals, idxs, hots = [], [], []
    for _ in range(TOP_K):
        m = jnp.max(work, axis=1, keepdims=True)
        sel = jnp.min(jnp.where(work == m, lane, LANES), axis=1, keepdims=True)
        hot = lane == sel
        work = jnp.where(hot, -jnp.inf, work)
        vals.append(m)
        idxs.append(sel)
        hots.append(hot)
    exps = [jnp.exp(vv - vals[0]) for vv in vals]
    denom = exps[0] + exps[1] + exps[2] + exps[3]
    gates = [e / denom for e in exps]

    hot_any = (hots[0] | hots[1] | hots[2] | hots[3]).astype(F32)
    r_idx = lax.broadcasted_iota(jnp.int32, (tm, tm), 0)
    c_idx = lax.broadcasted_iota(jnp.int32, (tm, tm), 1)
    earlier = jnp.where(c_idx < r_idx, 1.0, 0.0).astype(BF16)
    prefix = jnp.dot(earlier, hot_any.astype(BF16), preferred_element_type=F32) + counts[0:1, :]
    ranks = [jnp.sum(jnp.where(hot, prefix, 0.0), axis=1, keepdims=True).astype(jnp.int32) for hot in hots]
    counts[...] = counts[...] + jnp.sum(hot_any, axis=0, keepdims=True)

    idx_ref[...] = _lane_pack(idxs, tm)
    gate_ref[...] = _lane_pack(gates, tm)
    rank_ref[...] = _lane_pack(ranks, tm)
    cnt_ref[...] = counts[...]


def out_projection(l, s5_out, attn, x, g1, sc2, sh2, w_out_bf16, ln_g, ln_b, w_router_pad, b_router_pad, tm):
    b, L, _ = x.shape
    t = b * L
    nj = L // tm
    kern = functools.partial(_outproj_kernel, tm=tm)
    mod_spec = pl.BlockSpec((1, 1, D_MODEL), lambda i, j, l_ref: (i, 0, 0))
    vec_spec = pl.BlockSpec((1, D_MODEL), lambda i, j, l_ref: (0, 0))
    half_spec = pl.BlockSpec((1, tm, S5_WIDTH), lambda i, j, l_ref: (i, j, 0))
    full_spec = pl.BlockSpec((1, tm, D_MODEL), lambda i, j, l_ref: (i, j, 0))
    tok_spec = pl.BlockSpec((tm, LANES), lambda i, j, l_ref: (i * nj + j, 0))
    return pl.pallas_call(
        kern,
        grid_spec=pltpu.PrefetchScalarGridSpec(
            num_scalar_prefetch=1,
            grid=(b, nj),
            in_specs=[half_spec, half_spec, full_spec, mod_spec, mod_spec, mod_spec,
                      pl.BlockSpec((1, D_MODEL, D_MODEL), lambda i, j, l_ref: (l_ref[0], 0, 0)),
                      vec_spec, vec_spec,
                      pl.BlockSpec((D_MODEL, LANES), lambda i, j, l_ref: (0, 0)),
                      pl.BlockSpec((1, LANES), lambda i, j, l_ref: (0, 0))],
            out_specs=[full_spec,
                       pl.BlockSpec((tm * ROW_TILES, LANES), lambda i, j, l_ref: (i * nj + j, 0)),
                       tok_spec, tok_spec, tok_spec,
                       pl.BlockSpec((SUBLANES, LANES), lambda i, j, l_ref: (0, 0))],
            scratch_shapes=[pltpu.VMEM((SUBLANES, LANES), F32)]),
        out_shape=[jax.ShapeDtypeStruct((b, L, D_MODEL), F32),
                   jax.ShapeDtypeStruct((t * ROW_TILES, LANES), F32),
                   jax.ShapeDtypeStruct((t, LANES), jnp.int32),
                   jax.ShapeDtypeStruct((t, LANES), F32),
                   jax.ShapeDtypeStruct((t, LANES), jnp.int32),
                   jax.ShapeDtypeStruct((SUBLANES, LANES), F32)],
        compiler_params=_cparams(2),
        name="out_proj_router",
    )(l, s5_out, attn, x, g1, sc2, sh2, w_out_bf16, ln_g, ln_b, w_router_pad, b_router_pad)


def _row_copy(src, src_row, dst, dst_row, sem):
    return pltpu.make_async_copy(
        src.at[pl.ds(pl.multiple_of(src_row * ROW_TILES, ROW_TILES), ROW_TILES)],
        dst.at[pl.ds(pl.multiple_of(dst_row * ROW_TILES, ROW_TILES), ROW_TILES)], sem)


def _dispatch_kernel(lo_ref, hi_ref, dest_ref, h2_ref, xs_ref, zero_tile, sem, fill_sem, *, tm, n_fill):
    tok0 = pl.program_id(0) * tm
    n = tm * TOP_K

    @pl.when(pl.program_id(0) == 0)
    def _():
        zero_tile[...] = jnp.zeros_like(zero_tile)

        def fill_range(e, _):
            def fill_row(r, _):
                _row_copy(zero_tile, 0, xs_ref, r, fill_sem).start()
                return 0
            lax.fori_loop(lo_ref[e], hi_ref[e], fill_row, 0)
            return 0

        lax.fori_loop(0, N_EXPERTS + 1, fill_range, 0)

        def fill_drain(r, _):
            _row_copy(zero_tile, 0, xs_ref, 0, fill_sem).wait()
            return 0

        lax.fori_loop(0, n_fill, fill_drain, 0)

    def issue(a, _):
        _row_copy(h2_ref, tok0 + a // TOP_K, xs_ref, dest_ref[a], sem).start()
        return 0

    lax.fori_loop(0, n, issue, 0)

    def drain(a, _):
        _row_copy(h2_ref, tok0 + a // TOP_K, xs_ref, dest_ref[a], sem).wait()
        return 0

    lax.fori_loop(0, n, drain, 0)


def moe_dispatch(fill_lo, fill_hi, dest_flat, h2_tiles, n_rows, tm):
    n_assign = dest_flat.shape[0]
    kern = functools.partial(_dispatch_kernel, tm=tm, n_fill=n_rows - n_assign)
    return pl.pallas_call(
        kern,
        grid_spec=pltpu.PrefetchScalarGridSpec(
            num_scalar_prefetch=2,
            grid=(n_assign // (tm * TOP_K),),
            in_specs=[pl.BlockSpec((tm * TOP_K,), lambda i, lo, hi: (i,), memory_space=pltpu.SMEM),
                      pl.BlockSpec(memory_space=pl.ANY)],
            out_specs=pl.BlockSpec(memory_space=pl.ANY),
            scratch_shapes=[pltpu.VMEM((ROW_TILES, LANES), F32),
                            pltpu.SemaphoreType.DMA(()), pltpu.SemaphoreType.DMA(())]),
        out_shape=jax.ShapeDtypeStruct((n_rows * ROW_TILES, LANES), F32),
        compiler_params=_cparams(1),
        name="moe_dispatch",
    )(fill_lo, fill_hi, dest_flat, h2_tiles)


def _expert_kernel(l_ref, be_ref, nv_ref, xs_ref, wgu_ref, bgu_ref, wd_ref, bd_ref, ys_ref, *, rb):
    del l_ref, be_ref
    nvalid = nv_ref[pl.program_id(0)]

    @pl.when(nvalid > 0)
    def _():
        x = jnp.concatenate([xs_ref[pl.ds(s, rb, stride=ROW_TILES), :] for s in range(ROW_TILES)], axis=1)
        gu = jnp.dot(x.astype(BF16), wgu_ref[0, 0], preferred_element_type=F32) + bgu_ref[0, 0]
        gate = jnp.minimum(gu[:, :D_FF], SWIGLU_LIMIT)
        up = jnp.clip(gu[:, D_FF:], -SWIGLU_LIMIT, SWIGLU_LIMIT)
        act = (up + 1.0) * gate * jax.nn.sigmoid(SWIGLU_ALPHA * gate)
        y = jnp.dot(act.astype(BF16), wd_ref[0, 0], preferred_element_type=F32) + bd_ref[0, 0]
        for s in range(ROW_TILES):
            ys_ref[pl.ds(s, rb, stride=ROW_TILES), :] = y[:, s * LANES:(s + 1) * LANES]

    @pl.when(nvalid <= 0)
    def _():
        ys_ref[...] = jnp.zeros_like(ys_ref)


def moe_experts(l, block_expert, block_valid, xs_tiles, wgu_bf16, b_gate_up, wd_bf16, b_down, rb):
    n_blocks = block_expert.shape[0]
    kern = functools.partial(_expert_kernel, rb=rb)
    row_spec = pl.BlockSpec((rb * ROW_TILES, LANES), lambda i, l_ref, be, nv: (i, 0))
    return pl.pallas_call(
        kern,
        grid_spec=pltpu.PrefetchScalarGridSpec(
            num_scalar_prefetch=3,
            grid=(n_blocks,),
            in_specs=[row_spec,
                      pl.BlockSpec((1, 1, D_MODEL, 2 * D_FF), lambda i, l_ref, be, nv: (l_ref[0], be[i], 0, 0)),
                      pl.BlockSpec((1, 1, 1, 2 * D_FF), lambda i, l_ref, be, nv: (l_ref[0], be[i], 0, 0)),
                      pl.BlockSpec((1, 1, D_FF, D_MODEL), lambda i, l_ref, be, nv: (l_ref[0], be[i], 0, 0)),
                      pl.BlockSpec((1, 1, 1, D_MODEL), lambda i, l_ref, be, nv: (l_ref[0], be[i], 0, 0))],
            out_specs=row_spec),
        out_shape=jax.ShapeDtypeStruct(xs_tiles.shape, F32),
        compiler_params=_cparams(1),
        name="moe_experts",
    )(l, block_expert, block_valid, xs_tiles, wgu_bf16,
      b_gate_up.reshape(DEPTH, N_EXPERTS, 1, 2 * D_FF), wd_bf16, b_down.reshape(DEPTH, N_EXPERTS, 1, D_MODEL))


def _combine_kernel(dest_ref, ys_ref, gate_ref, x1_ref, g2_ref, lg_ref, lb_ref, o_ref, buf, sem, *, tm):
    n = tm * TOP_K

    def slot(a):
        return (a % TOP_K) * tm + a // TOP_K

    def issue(a, _):
        _row_copy(ys_ref, dest_ref[a], buf, slot(a), sem).start()
        return 0

    lax.fori_loop(0, n, issue, 0)

    def drain(a, _):
        _row_copy(ys_ref, dest_ref[a], buf, slot(a), sem).wait()
        return 0

    lax.fori_loop(0, n, drain, 0)

    ff = jnp.zeros((tm, D_MODEL), F32)
    for kk in range(TOP_K):
        rows = jnp.concatenate(
            [buf[pl.ds(kk * tm * ROW_TILES + s, tm, stride=ROW_TILES), :] for s in range(ROW_TILES)], axis=1)
        ff = ff + rows * gate_ref[:, kk:kk + 1]
    o_ref[0] = _layer_norm(DEEPNORM_ALPHA * x1_ref[0] + (1.0 + g2_ref[0]) * ff, lg_ref[...], lb_ref[...])


def moe_combine(dest_flat, ys_tiles, gates, x1, g2, ln_g, ln_b, tm):
    b, L, _ = x1.shape
    nj = L // tm
    kern = functools.partial(_combine_kernel, tm=tm)
    full_spec = pl.BlockSpec((1, tm, D_MODEL), lambda i, j: (i, j, 0))
    vec_spec = pl.BlockSpec((1, D_MODEL), lambda i, j: (0, 0))
    return pl.pallas_call(
        kern,
        grid=(b, nj),
        in_specs=[pl.BlockSpec((tm * TOP_K,), lambda i, j: (i * nj + j,), memory_space=pltpu.SMEM),
                  pl.BlockSpec(memory_space=pl.ANY),
                  pl.BlockSpec((tm, LANES), lambda i, j: (i * nj + j, 0)),
                  full_spec,
                  pl.BlockSpec((1, 1, D_MODEL), lambda i, j: (i, 0, 0)),
                  vec_spec, vec_spec],
        out_specs=full_spec,
        out_shape=jax.ShapeDtypeStruct((b, L, D_MODEL), F32),
        scratch_shapes=[pltpu.VMEM((tm * TOP_K * ROW_TILES, LANES), F32), pltpu.SemaphoreType.DMA(())],
        compiler_params=_cparams(2),
        name="moe_combine",
    )(dest_flat, ys_tiles, gates, x1, g2, ln_g, ln_b)


def _routing_tables(counts, top_idx, rank, rb, n_blocks):
    counts = counts.astype(jnp.int32)
    padded = (counts + rb - 1) // rb * rb
    pend = jnp.cumsum(padded)
    pstart = pend - padded
    dest = (pstart[top_idx] + rank).reshape(-1)
    row0 = jnp.arange(n_blocks, dtype=jnp.int32) * rb
    block_expert = jnp.minimum(jnp.searchsorted(pend, row0, side='right'), N_EXPERTS - 1).astype(jnp.int32)
    block_valid = jnp.clip(counts[block_expert] - (row0 - pstart[block_expert]), 0, rb).astype(jnp.int32)
    tail = jnp.full((1,), n_blocks * rb, jnp.int32)
    fill_lo = jnp.concatenate([pstart + counts, pend[-1:]]).astype(jnp.int32)
    fill_hi = jnp.concatenate([pend, tail]).astype(jnp.int32)
    return dest.astype(jnp.int32), block_expert, block_valid, fill_lo, fill_hi


def _tiles(b, L):
    t = b * L
    return dict(
        tm_proj=min(L, 512),
        lc=min(L, 64),
        tq=min(L, 128),
        tm_out=min(L, 256),
        tm_moe=min(L, 128),
        rb=256 if t >= 8192 else 64,
    )


def _run_layer(l, x, mod, k_past, v_past, s0_re, s0_im, prm):
    b, L, _ = x.shape
    t = b * L
    tl = _tiles(b, L)
    sh1, sc1, g1, sh2, sc2, g2 = [m[:, None, :] for m in jnp.split(mod, 6, axis=-1)]

    u, q, k, v = in_projection(l, x, sc1, sh1, prm['w_in'], tl['tm_proj'])
    s5_out, s_re, s_im = s5_mixer(l, u, s0_re, s0_im, prm['wb'], prm['ab_re'], prm['ab_im'], prm['wc'],
                                  prm['s5_d'], prm['w_glu'], prm['b_glu'], tl['lc'])
    if k_past is None:
        attn = stick_breaking(q, k, v, tl['tq'], 128, 0)
    else:
        past = k_past.shape[2]
        tk = 128
        total = -(-(past + L) // tk) * tk
        pad = jnp.zeros((b, SB_HEADS, total - past - L, SB_HEAD_DIM), F32)
        k_all = jnp.concatenate([k_past.astype(F32), k, pad], axis=2)
        v_all = jnp.concatenate([v_past.astype(F32), v, pad], axis=2)
        attn = stick_breaking(q, k_all, v_all, tl['tq'], tk, past)

    x1, h2_tiles, top_idx, gates, rank, counts = out_projection(
        l, s5_out, attn, x, g1, sc2, sh2, prm['w_out'], prm['ln1_g'], prm['ln1_b'],
        prm['w_router'], prm['b_router'], tl['tm_out'])

    rb = tl['rb']
    n_assign = t * TOP_K
    n_blocks = (n_assign + N_EXPERTS * (rb - 1) + rb - 1) // rb
    dest, block_expert, block_valid, fill_lo, fill_hi = _routing_tables(
        counts[0, :N_EXPERTS], top_idx[:, :TOP_K], rank[:, :TOP_K], rb, n_blocks)
    xs_tiles = moe_dispatch(fill_lo, fill_hi, dest, h2_tiles, n_blocks * rb, tl['tm_moe'])
    ys_tiles = moe_experts(l, block_expert, block_valid, xs_tiles, prm['w_gate_up'], prm['b_gate_up'],
                           prm['w_down'], prm['b_down'], rb)
    x2 = moe_combine(dest, ys_tiles, gates, x1, g2, prm['ln2_g'], prm['ln2_b'], tl['tm_moe'])

    hshape = (b, S5_GROUPS, S5_STATE)
    return x2, k, v, s_re.reshape(hshape), s_im.reshape(hshape)


def kernel(x_prompt, x_sample, cache_k, cache_v, state_s5_re, state_s5_im, c_prompt, c_sample, ln_in_g, ln_in_b, w_ada, b_ada, w_in, s5_a_re, s5_a_im, s5_log_dt, s5_b_re, s5_b_im, s5_c_re, s5_c_im, s5_d, s5_w_glu, s5_b_glu, w_out, ln1_g, ln1_b, w_router, b_router, w_gate_up, b_gate_up, w_down, b_down, ln2_g, ln2_b):
    bp, lp, _ = x_prompt.shape
    bs, ls, _ = x_sample.shape

    mod_all = adaln_all(jnp.concatenate([c_prompt, c_sample], axis=0), w_ada, b_ada)
    xp = ln_rows(x_prompt.reshape(bp * lp, D_MODEL), ln_in_g, ln_in_b, min(512, bp * lp)).reshape(bp, lp, D_MODEL)
    xs = ln_rows(x_sample.reshape(bs * ls, D_MODEL), ln_in_g, ln_in_b, min(512, bs * ls)).reshape(bs, ls, D_MODEL)

    w_in_bf = w_in.astype(BF16)
    w_out_bf = w_out.astype(BF16)
    w_glu_bf = s5_w_glu.astype(BF16)
    wgu_bf = w_gate_up.astype(BF16)
    wd_bf = w_down.astype(BF16)
    w_router_pad = jnp.pad(w_router.astype(F32), ((0, 0), (0, 0), (0, LANES - N_EXPERTS)))
    b_router_pad = jnp.pad(b_router.astype(F32), ((0, 0), (0, LANES - N_EXPERTS)), constant_values=-1e30)
    zeros_p = jnp.zeros((bp, S5_NSTATE), F32)

    kp, vp, srp, sip = [], [], [], []
    ksm, vsm, srs, sis = [], [], [], []
    for li in range(DEPTH):
        l = jnp.full((1,), li, jnp.int32)
        ab_re, ab_im, wb, wc = s5_discretise(s5_a_re[li], s5_a_im[li], s5_log_dt[li], s5_b_re[li], s5_b_im[li],
                                             s5_c_re[li], s5_c_im[li])
        prm = dict(w_in=w_in_bf, w_out=w_out_bf, w_glu=w_glu_bf, w_gate_up=wgu_bf, w_down=wd_bf,
                   b_gate_up=b_gate_up, b_down=b_down,
                   ab_re=ab_re, ab_im=ab_im, wb=wb, wc=wc,
                   s5_d=s5_d[li].reshape(1, S5_WIDTH), b_glu=s5_b_glu[li].reshape(1, S5_WIDTH),
                   ln1_g=ln1_g[li].reshape(1, D_MODEL), ln1_b=ln1_b[li].reshape(1, D_MODEL),
                   ln2_g=ln2_g[li].reshape(1, D_MODEL), ln2_b=ln2_b[li].reshape(1, D_MODEL),
                   w_router=w_router_pad[li], b_router=b_router_pad[li].reshape(1, LANES))
        xp, k, v, s_re, s_im = _run_layer(l, xp, mod_all[li, :bp], None, None, zeros_p, zeros_p, prm)
        kp.append(k); vp.append(v); srp.append(s_re); sip.append(s_im)
        xs, k, v, s_re, s_im = _run_layer(l, xs, mod_all[li, bp:], cache_k[li], cache_v[li],
                                          state_s5_re[li].reshape(bs, S5_NSTATE),
                                          state_s5_im[li].reshape(bs, S5_NSTATE), prm)
        ksm.append(k); vsm.append(v); srs.append(s_re); sis.append(s_im)
    return (xp, xs, jnp.stack(kp), jnp.stack(vp), jnp.stack(srp), jnp.stack(sip),
            jnp.stack(ksm), jnp.stack(vsm), jnp.stack(srs), jnp.stack(sis))
```

```python
import functools
import math

import jax
import jax.numpy as jnp
from jax import lax
from jax.experimental import pallas as pl
from jax.experimental.pallas import tpu as pltpu

F32 = jnp.float32
BF16 = jnp.bfloat16
HIGHEST = lax.Precision.HIGHEST

D_MODEL = 1024
DEPTH = 4
S5_WIDTH = D_MODEL // 2
S5_GROUP_CH = 16
S5_GROUPS = S5_WIDTH // S5_GROUP_CH
S5_STATE = 64
S5_NSTATE = S5_GROUPS * S5_STATE
SB_WIDTH = D_MODEL - S5_WIDTH
SB_HEAD_DIM = 64
SB_HEADS = SB_WIDTH // SB_HEAD_DIM
N_EXPERTS = 32
TOP_K = 4
D_FF = D_MODEL
SWIGLU_LIMIT = 7.0
SWIGLU_ALPHA = 1.702
LN_EPS = 1e-5
DEEPNORM_ALPHA = (2 * DEPTH) ** 0.25

LANES = 128
SUBLANES = 8
ROW_TILES = D_MODEL // LANES
VMEM_LIMIT = 56 * 1024 * 1024


def _cparams(n_axes):
    return pltpu.CompilerParams(dimension_semantics=("arbitrary",) * n_axes,
                                vmem_limit_bytes=VMEM_LIMIT)


def _layer_norm(x, g, b):
    mu = jnp.mean(x, axis=-1, keepdims=True)
    xc = x - mu
    var = jnp.mean(xc * xc, axis=-1, keepdims=True)
    return xc * lax.rsqrt(var + LN_EPS) * g + b


def _ada_kernel(c_ref, w_ref, b_ref, o_ref):
    c = c_ref[...]
    s = c * jax.nn.sigmoid(c)
    o_ref[0] = jnp.dot(s, w_ref[0], precision=HIGHEST, preferred_element_type=F32) + b_ref[0]


def adaln_all(c_all, w_ada, b_ada):
    bc = c_all.shape[0]
    return pl.pallas_call(
        _ada_kernel,
        grid=(DEPTH, 6),
        in_specs=[pl.BlockSpec((bc, D_MODEL), lambda l, j: (0, 0)),
                  pl.BlockSpec((1, D_MODEL, D_MODEL), lambda l, j: (l, 0, j)),
                  pl.BlockSpec((1, 1, D_MODEL), lambda l, j: (l, 0, j))],
        out_specs=pl.BlockSpec((1, bc, D_MODEL), lambda l, j: (l, 0, j)),
        out_shape=jax.ShapeDtypeStruct((DEPTH, bc, 6 * D_MODEL), F32),
        compiler_params=_cparams(2),
        name="adaln",
    )(c_all, w_ada, b_ada.reshape(DEPTH, 1, 6 * D_MODEL))


def _ln_kernel(x_ref, g_ref, b_ref, o_ref):
    o_ref[...] = _layer_norm(x_ref[...], g_ref[...], b_ref[...])


def ln_rows(x2d, g, b, tm):
    t = x2d.shape[0]
    return pl.pallas_call(
        _ln_kernel,
        grid=(t // tm,),
        in_specs=[pl.BlockSpec((tm, D_MODEL), lambda i: (i, 0)),
                  pl.BlockSpec((1, D_MODEL), lambda i: (0, 0)),
                  pl.BlockSpec((1, D_MODEL), lambda i: (0, 0))],
        out_specs=pl.BlockSpec((tm, D_MODEL), lambda i: (i, 0)),
        out_shape=jax.ShapeDtypeStruct((t, D_MODEL), F32),
        compiler_params=_cparams(1),
        name="ln_in",
    )(x2d, g.reshape(1, D_MODEL), b.reshape(1, D_MODEL))


def _inproj_kernel(l_ref, x_ref, sc_ref, sh_ref, w_ref, u_ref, q_ref, k_ref, v_ref):
    del l_ref
    h = x_ref[0] * (1.0 + sc_ref[0]) + sh_ref[0]
    proj = jnp.dot(h.astype(BF16), w_ref[0], preferred_element_type=F32)
    u_ref[0] = proj[:, :S5_WIDTH]
    for hh in range(SB_HEADS):
        lo = S5_WIDTH + hh * SB_HEAD_DIM
        q_ref[0, hh] = proj[:, lo:lo + SB_HEAD_DIM]
        k_ref[0, hh] = proj[:, lo + SB_WIDTH:lo + SB_WIDTH + SB_HEAD_DIM]
        v_ref[0, hh] = proj[:, lo + 2 * SB_WIDTH:lo + 2 * SB_WIDTH + SB_HEAD_DIM]


def in_projection(l, x, sc1, sh1, w_in_bf16, tm):
    b, L, _ = x.shape
    n_proj = S5_WIDTH + 3 * SB_WIDTH
    mod_spec = pl.BlockSpec((1, 1, D_MODEL), lambda i, j, l_ref: (i, 0, 0))
    head_spec = pl.BlockSpec((1, SB_HEADS, tm, SB_HEAD_DIM), lambda i, j, l_ref: (i, 0, j, 0))
    head_shape = jax.ShapeDtypeStruct((b, SB_HEADS, L, SB_HEAD_DIM), F32)
    return pl.pallas_call(
        _inproj_kernel,
        grid_spec=pltpu.PrefetchScalarGridSpec(
            num_scalar_prefetch=1,
            grid=(b, L // tm),
            in_specs=[pl.BlockSpec((1, tm, D_MODEL), lambda i, j, l_ref: (i, j, 0)),
                      mod_spec, mod_spec,
                      pl.BlockSpec((1, D_MODEL, n_proj), lambda i, j, l_ref: (l_ref[0], 0, 0))],
            out_specs=[pl.BlockSpec((1, tm, S5_WIDTH), lambda i, j, l_ref: (i, j, 0)),
                       head_spec, head_spec, head_spec]),
        out_shape=[jax.ShapeDtypeStruct((b, L, S5_WIDTH), F32), head_shape, head_shape, head_shape],
        compiler_params=_cparams(2),
        name="in_proj",
    )(l, x, sc1, sh1, w_in_bf16)


S5_BATCH = SUBLANES
S5_SLABS = S5_NSTATE // LANES
S5_CHUNKS = S5_WIDTH // LANES
S5_CHUNK_STATES = S5_NSTATE // S5_CHUNKS
S5_CHUNK_SLABS = S5_CHUNK_STATES // LANES


def _s5_kernel(l_ref, u_ref, x0r_ref, x0i_ref, wb_ref, ar_ref, ai_ref, wc_ref, d_ref, wg_ref, bg_ref,
               o_ref, sr_ref, si_ref, upad, slab, st_re, st_im, *, lc, pitch):
    del l_ref
    ci = pl.program_id(1)

    @pl.when(ci == 0)
    def _():
        upad[...] = jnp.zeros_like(upad)
        st_re[...] = x0r_ref[...]
        st_im[...] = x0i_ref[...]

    for b in range(S5_BATCH):
        upad[b * pitch:b * pitch + lc, :] = u_ref[b]

    for c in range(S5_CHUNKS):
        r = jnp.dot(upad[:, c * LANES:(c + 1) * LANES].astype(BF16), wb_ref[c],
                    preferred_element_type=F32)
        for j in range(S5_CHUNK_SLABS):
            slab[c * S5_CHUNK_SLABS + j] = r[:, j * LANES:(j + 1) * LANES]
            slab[S5_SLABS + c * S5_CHUNK_SLABS + j] = r[:, S5_CHUNK_STATES + j * LANES:
                                                          S5_CHUNK_STATES + (j + 1) * LANES]

    for c in range(S5_CHUNKS):
        lane0 = c * S5_CHUNK_STATES
        ar = [jnp.broadcast_to(ar_ref[:, lane0 + j * LANES:lane0 + (j + 1) * LANES], (S5_BATCH, LANES))
              for j in range(S5_CHUNK_SLABS)]
        ai = [jnp.broadcast_to(ai_ref[:, lane0 + j * LANES:lane0 + (j + 1) * LANES], (S5_BATCH, LANES))
              for j in range(S5_CHUNK_SLABS)]
        xr0 = tuple(st_re[:, lane0 + j * LANES:lane0 + (j + 1) * LANES] for j in range(S5_CHUNK_SLABS))
        xi0 = tuple(st_im[:, lane0 + j * LANES:lane0 + (j + 1) * LANES] for j in range(S5_CHUNK_SLABS))

        def step(t, carry, c=c, ar=ar, ai=ai):
            xr, xi = carry
            nxr, nxi = [], []
            for j in range(S5_CHUNK_SLABS):
                s_re = c * S5_CHUNK_SLABS + j
                s_im = S5_SLABS + s_re
                rows_t = pl.ds(t, S5_BATCH, stride=pitch)
                nr = ar[j] * xr[j] - ai[j] * xi[j] + slab[s_re, rows_t, :]
                ni = ar[j] * xi[j] + ai[j] * xr[j] + slab[s_im, rows_t, :]
                slab[s_re, rows_t, :] = nr
                slab[s_im, rows_t, :] = ni
                nxr.append(nr)
                nxi.append(ni)
            return tuple(nxr), tuple(nxi)

        xr, xi = lax.fori_loop(0, lc, step, (xr0, xi0))
        for j in range(S5_CHUNK_SLABS):
            st_re[:, lane0 + j * LANES:lane0 + (j + 1) * LANES] = xr[j]
            st_im[:, lane0 + j * LANES:lane0 + (j + 1) * LANES] = xi[j]

    sr_ref[...] = st_re[...]
    si_ref[...] = st_im[...]

    ys = []
    for c in range(S5_CHUNKS):
        xs = jnp.concatenate(
            [slab[c * S5_CHUNK_SLABS + j] for j in range(S5_CHUNK_SLABS)]
            + [slab[S5_SLABS + c * S5_CHUNK_SLABS + j] for j in range(S5_CHUNK_SLABS)], axis=1)
        ys.append(jnp.dot(xs.astype(BF16), wc_ref[c], preferred_element_type=F32))
    y = jnp.concatenate(ys, axis=1) + d_ref[...] * upad[...]
    g = jax.nn.gelu(y)
    gate = jnp.dot(g.astype(BF16), wg_ref[0], preferred_element_type=F32) + bg_ref[...]
    out = g * jax.nn.sigmoid(gate)
    for b in range(S5_BATCH):
        o_ref[b] = out[b * pitch:b * pitch + lc, :]


def s5_mixer(l, u, x0_re, x0_im, wb, ab_re, ab_im, wc, d, w_glu_bf16, b_glu, lc):
    b, L, _ = u.shape
    pitch = lc + SUBLANES
    rows = S5_BATCH * pitch
    kern = functools.partial(_s5_kernel, lc=lc, pitch=pitch)
    const2 = lambda i, j, l_ref: (0, 0)
    const3 = lambda i, j, l_ref: (0, 0, 0)
    state_spec = pl.BlockSpec((S5_BATCH, S5_NSTATE), lambda i, j, l_ref: (i, 0))
    return pl.pallas_call(
        kern,
        grid_spec=pltpu.PrefetchScalarGridSpec(
            num_scalar_prefetch=1,
            grid=(b // S5_BATCH, L // lc),
            in_specs=[pl.BlockSpec((S5_BATCH, lc, S5_WIDTH), lambda i, j, l_ref: (i, j, 0)),
                      state_spec, state_spec,
                      pl.BlockSpec((S5_CHUNKS, LANES, 2 * S5_CHUNK_STATES), const3),
                      pl.BlockSpec((1, S5_NSTATE), const2),
                      pl.BlockSpec((1, S5_NSTATE), const2),
                      pl.BlockSpec((S5_CHUNKS, 2 * S5_CHUNK_STATES, LANES), const3),
                      pl.BlockSpec((1, S5_WIDTH), const2),
                      pl.BlockSpec((1, S5_WIDTH, S5_WIDTH), lambda i, j, l_ref: (l_ref[0], 0, 0)),
                      pl.BlockSpec((1, S5_WIDTH), const2)],
            out_specs=[pl.BlockSpec((S5_BATCH, lc, S5_WIDTH), lambda i, j, l_ref: (i, j, 0)),
                       state_spec, state_spec],
            scratch_shapes=[pltpu.VMEM((rows, S5_WIDTH), F32),
                            pltpu.VMEM((2 * S5_SLABS, rows, LANES), F32),
                            pltpu.VMEM((S5_BATCH, S5_NSTATE), F32),
                            pltpu.VMEM((S5_BATCH, S5_NSTATE), F32)]),
        out_shape=[jax.ShapeDtypeStruct((b, L, S5_WIDTH), F32),
                   jax.ShapeDtypeStruct((b, S5_NSTATE), F32),
                   jax.ShapeDtypeStruct((b, S5_NSTATE), F32)],
        compiler_params=_cparams(2),
        name="s5_mixer",
    )(l, u, x0_re, x0_im, wb, ab_re, ab_im, wc, d, w_glu_bf16, b_glu)


def s5_discretise(a_re, a_im, log_dt, b_re, b_im, c_re, c_im):
    a_re, a_im = a_re.astype(F32), a_im.astype(F32)
    dt = jnp.exp(log_dt.astype(F32))[:, None]
    mag = jnp.exp(a_re * dt)
    ab_re = mag * jnp.cos(a_im * dt)
    ab_im = mag * jnp.sin(a_im * dt)
    den = a_re * a_re + a_im * a_im
    nr, ni = ab_re - 1.0, ab_im
    f_re = (nr * a_re + ni * a_im) / den
    f_im = (ni * a_re - nr * a_im) / den
    b_re, b_im = b_re.astype(F32), b_im.astype(F32)
    bb_re = f_re[..., None] * b_re - f_im[..., None] * b_im
    bb_im = f_re[..., None] * b_im + f_im[..., None] * b_re
    gpc = S5_GROUPS // S5_CHUNKS
    eye = jnp.eye(gpc, dtype=F32)

    def blockdiag_in(bb):
        bb = bb.reshape(S5_CHUNKS, gpc, S5_STATE, S5_GROUP_CH)
        m = jnp.einsum('cgpn,gh->cgnhp', bb, eye)
        return m.reshape(S5_CHUNKS, gpc * S5_GROUP_CH, gpc * S5_STATE)

    def blockdiag_out(cc):
        cc = cc.reshape(S5_CHUNKS, gpc, S5_GROUP_CH, S5_STATE)
        m = jnp.einsum('cgnp,gh->cgphn', cc, eye)
        return m.reshape(S5_CHUNKS, gpc * S5_STATE, gpc * S5_GROUP_CH)

    wb = jnp.concatenate([blockdiag_in(bb_re), blockdiag_in(bb_im)], axis=2).astype(BF16)
    wc = jnp.concatenate([blockdiag_out(c_re.astype(F32)), -blockdiag_out(c_im.astype(F32))],
                         axis=1).astype(BF16)
    return ab_re.reshape(1, S5_NSTATE), ab_im.reshape(1, S5_NSTATE), wb, wc


ATTN_HEADS_PER_STEP = LANES // SB_HEAD_DIM
UNDERFLOW_LOG = -104.0


def _attn_kernel(q_ref, k_ref, v_ref, o_ref, *, tq, tk, qoff):
    q0 = pl.program_id(2) * tq
    jd = (qoff + q0) // tk
    scale = SB_HEAD_DIM ** -0.5
    s_idx = lax.broadcasted_iota(jnp.int32, (2 * tk, 2 * tk), 0) % tk
    j_idx = lax.broadcasted_iota(jnp.int32, (2 * tk, 2 * tk), 1)
    suffix_and_total = jnp.where((s_idx > j_idx) | (j_idx >= tk), 1.0, 0.0).astype(BF16)
    q_pos = qoff + q0 + lax.broadcasted_iota(jnp.int32, (tq, tk), 0)
    k_off = lax.broadcasted_iota(jnp.int32, (tq, tk), 1)
    heads = range(ATTN_HEADS_PER_STEP)
    qs = [(q_ref[0, hh] * scale).astype(BF16) for hh in heads]

    def scores(hh, j):
        kb = k_ref[0, hh, pl.ds(pl.multiple_of(j * tk, tk), tk), :].astype(BF16)
        return lax.dot_general(qs[hh], kb, (((1,), (1,)), ((), ())), preferred_element_type=F32)

    def values(hh, j):
        return v_ref[0, hh, pl.ds(pl.multiple_of(j * tk, tk), tk), :].astype(BF16)

    def block_terms(z, mask):
        sp = jnp.log(1.0 + jnp.exp(-jnp.abs(z)))
        log_not = -(jnp.maximum(z, 0.0) + sp)
        log_beta = jnp.minimum(z, 0.0) - sp
        if mask is not None:
            log_not = jnp.where(mask, log_not, 0.0)
        hi = log_not.astype(BF16)
        lo = (log_not - hi.astype(F32)).astype(BF16)
        cs = jnp.dot(jnp.concatenate([hi, lo], axis=1), suffix_and_total, preferred_element_type=F32)
        return log_beta, cs[:, :tk], cs[:, tk:]

    mask = (k_off + jd * tk) < q_pos
    carries, accs = [], []
    for hh in heads:
        log_beta, later, total = block_terms(scores(hh, jd), mask)
        w = jnp.where(mask, jnp.exp(log_beta + later), 0.0)
        accs.append(jnp.dot(w.astype(BF16), values(hh, jd), preferred_element_type=F32))
        carries.append(total)

    def live(carries):
        return jnp.max(jnp.maximum(carries[0], carries[1])) > UNDERFLOW_LOG

    def cond(state):
        it, _, _, go = state
        return (it < (jd + 1) // 2) & go

    def body(state):
        it, carries, accs, _ = state
        ja = jd - 1 - 2 * it
        has_b = ja >= 1
        jb = jnp.maximum(ja - 1, 0)
        new_carries, new_accs = [], []
        for hh in heads:
            lb_a, later_a, total_a = block_terms(scores(hh, ja), None)
            lb_b, later_b, total_b = block_terms(scores(hh, jb), None)
            carry_b = carries[hh] + total_a
            w_a = jnp.exp(lb_a + later_a + carries[hh]).astype(BF16)
            w_b = jnp.exp(lb_b + later_b + carry_b).astype(BF16)
            w_b = jnp.where(has_b, w_b, jnp.zeros_like(w_b))
            new_accs.append(accs[hh]
                            + jnp.dot(w_a, values(hh, ja), preferred_element_type=F32)
                            + jnp.dot(w_b, values(hh, jb), preferred_element_type=F32))
            new_carries.append(carry_b + jnp.where(has_b, total_b, 0.0))
        return it + 1, tuple(new_carries), tuple(new_accs), live(new_carries)

    _, _, accs, _ = lax.while_loop(cond, body, (jnp.int32(0), tuple(carries), tuple(accs), live(carries)))
    o_ref[0] = jnp.concatenate(accs, axis=1)


def stick_breaking(q, k, v, tq, tk, qoff):
    b, h, L, dh = q.shape
    tkeys = k.shape[2]
    assert tq <= tk and tk % tq == 0 and qoff % tk == 0 and tkeys % tk == 0
    hp = ATTN_HEADS_PER_STEP
    kern = functools.partial(_attn_kernel, tq=tq, tk=tk, qoff=qoff)
    kv_spec = pl.BlockSpec((1, hp, tkeys, dh), lambda i, j, t: (i, j, 0, 0))
    return pl.pallas_call(
        kern,
        grid=(b, h // hp, L // tq),
        in_specs=[pl.BlockSpec((1, hp, tq, dh), lambda i, j, t: (i, j, t, 0)), kv_spec, kv_spec],
        out_specs=pl.BlockSpec((1, tq, hp * dh), lambda i, j, t: (i, t, j)),
        out_shape=jax.ShapeDtypeStruct((b, L, h * dh), F32),
        compiler_params=_cparams(3),
        name="stick_breaking",
    )(q, k, v)


def _lane_pack(cols, tm):
    lane = lax.broadcasted_iota(jnp.int32, (tm, LANES), 1)
    out = jnp.zeros((tm, LANES), cols[0].dtype)
    for kk, col in enumerate(cols):
        out = jnp.where(lane == kk, col, out)
    return out


def _outproj_kernel(l_ref, s5_ref, at_ref, x_ref, g1_ref, sc2_ref, sh2_ref, w_ref, lg_ref, lb_ref,
                    wr_ref, br_ref, x1_ref, h2_ref, idx_ref, gate_ref, rank_ref, cnt_ref, counts, *, tm):
    del l_ref
    first = (pl.program_id(0) == 0) & (pl.program_id(1) == 0)

    @pl.when(first)
    def _():
        counts[...] = jnp.zeros_like(counts)

    mix = (jnp.dot(s5_ref[0].astype(BF16), w_ref[0, :S5_WIDTH, :], preferred_element_type=F32)
           + jnp.dot(at_ref[0].astype(BF16), w_ref[0, S5_WIDTH:, :], preferred_element_type=F32))
    x1 = _layer_norm(DEEPNORM_ALPHA * x_ref[0] + (1.0 + g1_ref[0]) * mix, lg_ref[...], lb_ref[...])
    x1_ref[0] = x1
    h2 = x1 * (1.0 + sc2_ref[0]) + sh2_ref[0]
    for s in range(ROW_TILES):
        h2_ref[pl.ds(s, tm, stride=ROW_TILES), :] = h2[:, s * LANES:(s + 1) * LANES]

    logits = jnp.dot(h2, wr_ref[...], precision=HIGHEST, preferred_element_type=F32) + br_ref[...]
    lane = lax.broadcasted_iota(jnp.int32, (tm, LANES), 1)
    work = logits
    vals, idxs, hots = [], [], []
    for _ in range(TOP_K):
        m = jnp.max(work, axis=1, keepdims=True)
        sel = jnp.min(jnp.where(work == m, lane, LANES), axis=1, keepdims=True)
        hot = lane == sel
        work = jnp.where(hot, -jnp.inf, work)
        vals.append(m)
        idxs.append(sel)
        hots.append(hot)
    exps = [jnp.exp(vv - vals[0]) for vv in vals]
    denom = exps[0] + exps[1] + exps[2] + exps[3]
    gates = [e / denom for e in exps]

    hot_any = (hots[0] | hots[1] | hots[2] | hots[3]).astype(F32)
    r_idx = lax.broadcasted_iota(jnp.int32, (tm, tm), 0)
    c_idx = lax.broadcasted_iota(jnp.int32, (tm, tm), 1)
    earlier = jnp.where(c_idx < r_idx, 1.0, 0.0).astype(BF16)
    prefix = jnp.dot(earlier, hot_any.astype(BF16), preferred_element_type=F32) + counts[0:1, :]
    ranks = [jnp.sum(jnp.where(hot, prefix, 0.0), axis=1, keepdims=True).astype(jnp.int32) for hot in hots]
    counts[...] = counts[...] + jnp.sum(hot_any, axis=0, keepdims=True)

    idx_ref[...] = _lane_pack(idxs, tm)
    gate_ref[...] = _lane_pack(gates, tm)
    rank_ref[...] = _lane_pack(ranks, tm)
    cnt_ref[...] = counts[...]


def out_projection(l, s5_out, attn, x, g1, sc2, sh2, w_out_bf16, ln_g, ln_b, w_router_pad, b_router_pad, tm):
    b, L, _ = x.shape
    t = b * L
    nj = L // tm
    kern = functools.partial(_outproj_kernel, tm=tm)
    mod_spec = pl.BlockSpec((1, 1, D_MODEL), lambda i, j, l_ref: (i, 0, 0))
    vec_spec = pl.BlockSpec((1, D_MODEL), lambda i, j, l_ref: (0, 0))
    half_spec = pl.BlockSpec((1, tm, S5_WIDTH), lambda i, j, l_ref: (i, j, 0))
    full_spec = pl.BlockSpec((1, tm, D_MODEL), lambda i, j, l_ref: (i, j, 0))
    tok_spec = pl.BlockSpec((tm, LANES), lambda i, j, l_ref: (i * nj + j, 0))
    return pl.pallas_call(
        kern,
        grid_spec=pltpu.PrefetchScalarGridSpec(
            num_scalar_prefetch=1,
            grid=(b, nj),
            in_specs=[half_spec, half_spec, full_spec, mod_spec, mod_spec, mod_spec,
                      pl.BlockSpec((1, D_MODEL, D_MODEL), lambda i, j, l_ref: (l_ref[0], 0, 0)),
                      vec_spec, vec_spec,
                      pl.BlockSpec((D_MODEL, LANES), lambda i, j, l_ref: (0, 0)),
                      pl.BlockSpec((1, LANES), lambda i, j, l_ref: (0, 0))],
            out_specs=[full_spec,
                       pl.BlockSpec((tm * ROW_TILES, LANES), lambda i, j, l_ref: (i * nj + j, 0)),
                       tok_spec, tok_spec, tok_spec,
                       pl.BlockSpec((SUBLANES, LANES), lambda i, j, l_ref: (0, 0))],
            scratch_shapes=[pltpu.VMEM((SUBLANES, LANES), F32)]),
        out_shape=[jax.ShapeDtypeStruct((b, L, D_MODEL), F32),
                   jax.ShapeDtypeStruct((t * ROW_TILES, LANES), F32),
                   jax.ShapeDtypeStruct((t, LANES), jnp.int32),
                   jax.ShapeDtypeStruct((t, LANES), F32),
                   jax.ShapeDtypeStruct((t, LANES), jnp.int32),
                   jax.ShapeDtypeStruct((SUBLANES, LANES), F32)],
        compiler_params=_cparams(2),
        name="out_proj_router",
    )(l, s5_out, attn, x, g1, sc2, sh2, w_out_bf16, ln_g, ln_b, w_router_pad, b_router_pad)


def _row_copy(src, src_row, dst, dst_row, sem):
    return pltpu.make_async_copy(
        src.at[pl.ds(pl.multiple_of(src_row * ROW_TILES, ROW_TILES), ROW_TILES)],
        dst.at[pl.ds(pl.multiple_of(dst_row * ROW_TILES, ROW_TILES), ROW_TILES)], sem)


def _rows_wait(big_ref, n_rows, sem):
    span = big_ref.at[pl.ds(0, n_rows * ROW_TILES)]
    pltpu.make_async_copy(span, span, sem).wait()


def _dispatch_kernel(lo_ref, hi_ref, dest_ref, h2_ref, xs_ref, zero_tile, sem, fill_sem, *, tm, n_fill):
    n = tm * TOP_K

    @pl.when(pl.program_id(0) == 0)
    def _():
        zero_tile[...] = jnp.zeros_like(zero_tile)

        def fill_range(e, _):
            def fill_row(r, _):
                _row_copy(zero_tile, 0, xs_ref, r, fill_sem).start()
                return 0
            lax.fori_loop(lo_ref[e], hi_ref[e], fill_row, 0)
            return 0

        lax.fori_loop(0, N_EXPERTS + 1, fill_range, 0)
        _rows_wait(xs_ref, n_fill, fill_sem)

    def issue(a, _):
        _row_copy(h2_ref, a // TOP_K, xs_ref, dest_ref[a], sem).start()
        return 0

    lax.fori_loop(0, n, issue, 0, unroll=8)
    _rows_wait(xs_ref, n, sem)


def moe_dispatch(fill_lo, fill_hi, dest_flat, h2_tiles, n_rows, tm):
    n_assign = dest_flat.shape[0]
    kern = functools.partial(_dispatch_kernel, tm=tm, n_fill=n_rows - n_assign)
    return pl.pallas_call(
        kern,
        grid_spec=pltpu.PrefetchScalarGridSpec(
            num_scalar_prefetch=2,
            grid=(n_assign // (tm * TOP_K),),
            in_specs=[pl.BlockSpec((tm * TOP_K,), lambda i, lo, hi: (i,), memory_space=pltpu.SMEM),
                      pl.BlockSpec((tm * ROW_TILES, LANES), lambda i, lo, hi: (i, 0))],
            out_specs=pl.BlockSpec(memory_space=pl.ANY),
            scratch_shapes=[pltpu.VMEM((ROW_TILES, LANES), F32),
                            pltpu.SemaphoreType.DMA(()), pltpu.SemaphoreType.DMA(())]),
        out_shape=jax.ShapeDtypeStruct((n_rows * ROW_TILES, LANES), F32),
        compiler_params=_cparams(1),
        name="moe_dispatch",
    )(fill_lo, fill_hi, dest_flat, h2_tiles)


def _expert_kernel(l_ref, be_ref, nv_ref, xs_ref, wgu_ref, bgu_ref, wd_ref, bd_ref, ys_ref, *, rb):
    del l_ref, be_ref
    nvalid = nv_ref[pl.program_id(0)]

    @pl.when(nvalid > 0)
    def _():
        x = jnp.concatenate([xs_ref[pl.ds(s, rb, stride=ROW_TILES), :] for s in range(ROW_TILES)], axis=1)
        gu = jnp.dot(x.astype(BF16), wgu_ref[0, 0], preferred_element_type=F32) + bgu_ref[0, 0]
        gate = jnp.minimum(gu[:, :D_FF], SWIGLU_LIMIT)
        up = jnp.clip(gu[:, D_FF:], -SWIGLU_LIMIT, SWIGLU_LIMIT)
        act = (up + 1.0) * gate * jax.nn.sigmoid(SWIGLU_ALPHA * gate)
        y = jnp.dot(act.astype(BF16), wd_ref[0, 0], preferred_element_type=F32) + bd_ref[0, 0]
        for s in range(ROW_TILES):
            ys_ref[pl.ds(s, rb, stride=ROW_TILES), :] = y[:, s * LANES:(s + 1) * LANES]

    @pl.when(nvalid <= 0)
    def _():
        ys_ref[...] = jnp.zeros_like(ys_ref)


def moe_experts(l, block_expert, block_valid, xs_tiles, wgu_bf16, b_gate_up, wd_bf16, b_down, rb):
    n_blocks = block_expert.shape[0]
    kern = functools.partial(_expert_kernel, rb=rb)
    row_spec = pl.BlockSpec((rb * ROW_TILES, LANES), lambda i, l_ref, be, nv: (i, 0))
    return pl.pallas_call(
        kern,
        grid_spec=pltpu.PrefetchScalarGridSpec(
            num_scalar_prefetch=3,
            grid=(n_blocks,),
            in_specs=[row_spec,
                      pl.BlockSpec((1, 1, D_MODEL, 2 * D_FF), lambda i, l_ref, be, nv: (l_ref[0], be[i], 0, 0)),
                      pl.BlockSpec((1, 1, 1, 2 * D_FF), lambda i, l_ref, be, nv: (l_ref[0], be[i], 0, 0)),
                      pl.BlockSpec((1, 1, D_FF, D_MODEL), lambda i, l_ref, be, nv: (l_ref[0], be[i], 0, 0)),
                      pl.BlockSpec((1, 1, 1, D_MODEL), lambda i, l_ref, be, nv: (l_ref[0], be[i], 0, 0))],
            out_specs=row_spec),
        out_shape=jax.ShapeDtypeStruct(xs_tiles.shape, F32),
        compiler_params=_cparams(1),
        name="moe_experts",
    )(l, block_expert, block_valid, xs_tiles, wgu_bf16,
      b_gate_up.reshape(DEPTH, N_EXPERTS, 1, 2 * D_FF), wd_bf16, b_down.reshape(DEPTH, N_EXPERTS, 1, D_MODEL))


def _combine_kernel(dest_ref, next_ref, ys_ref, gate_ref, x1_ref, g2_ref, lg_ref, lb_ref, o_ref, buf, sems,
                    *, tm, n_steps):
    n = tm * TOP_K
    step = pl.program_id(0) * pl.num_programs(1) + pl.program_id(1)
    half = step % 2

    def gather(table, into):
        def issue(a, _):
            slot = into * n + (a % TOP_K) * tm + a // TOP_K
            _row_copy(ys_ref, table[a], buf, slot, sems.at[into]).start()
            return 0
        lax.fori_loop(0, n, issue, 0, unroll=8)

    @pl.when(step == 0)
    def _():
        gather(dest_ref, half)

    @pl.when(step + 1 < n_steps)
    def _():
        gather(next_ref, 1 - half)

    _rows_wait(buf, n, sems.at[half])

    base = half * n * ROW_TILES
    ff = jnp.zeros((tm, D_MODEL), F32)
    for kk in range(TOP_K):
        rows = jnp.concatenate(
            [buf[pl.ds(base + kk * tm * ROW_TILES + s, tm, stride=ROW_TILES), :] for s in range(ROW_TILES)],
            axis=1)
        ff = ff + rows * gate_ref[:, kk:kk + 1]
    o_ref[0] = _layer_norm(DEEPNORM_ALPHA * x1_ref[0] + (1.0 + g2_ref[0]) * ff, lg_ref[...], lb_ref[...])


def moe_combine(dest_flat, ys_tiles, gates, x1, g2, ln_g, ln_b, tm):
    b, L, _ = x1.shape
    nj = L // tm
    n_steps = b * nj
    kern = functools.partial(_combine_kernel, tm=tm, n_steps=n_steps)
    full_spec = pl.BlockSpec((1, tm, D_MODEL), lambda i, j: (i, j, 0))
    vec_spec = pl.BlockSpec((1, D_MODEL), lambda i, j: (0, 0))
    return pl.pallas_call(
        kern,
        grid=(b, nj),
        in_specs=[pl.BlockSpec((tm * TOP_K,), lambda i, j: (i * nj + j,), memory_space=pltpu.SMEM),
                  pl.BlockSpec((tm * TOP_K,), lambda i, j: (jnp.minimum(i * nj + j + 1, n_steps - 1),),
                               memory_space=pltpu.SMEM),
                  pl.BlockSpec(memory_space=pl.ANY),
                  pl.BlockSpec((tm, LANES), lambda i, j: (i * nj + j, 0)),
                  full_spec,
                  pl.BlockSpec((1, 1, D_MODEL), lambda i, j: (i, 0, 0)),
                  vec_spec, vec_spec],
        out_specs=full_spec,
        out_shape=jax.ShapeDtypeStruct((b, L, D_MODEL), F32),
        scratch_shapes=[pltpu.VMEM((2 * tm * TOP_K * ROW_TILES, LANES), F32), pltpu.SemaphoreType.DMA((2,))],
        compiler_params=_cparams(2),
        name="moe_combine",
    )(dest_flat, dest_flat, ys_tiles, gates, x1, g2, ln_g, ln_b)


def _routing_tables(counts, top_idx, rank, rb, n_blocks):
    counts = counts.astype(jnp.int32)
    padded = (counts + rb - 1) // rb * rb
    pend = jnp.cumsum(padded)
    pstart = pend - padded
    dest = (pstart[top_idx] + rank).reshape(-1)
    row0 = jnp.arange(n_blocks, dtype=jnp.int32) * rb
    block_expert = jnp.minimum(jnp.searchsorted(pend, row0, side='right'), N_EXPERTS - 1).astype(jnp.int32)
    block_valid = jnp.clip(counts[block_expert] - (row0 - pstart[block_expert]), 0, rb).astype(jnp.int32)
    tail = jnp.full((1,), n_blocks * rb, jnp.int32)
    fill_lo = jnp.concatenate([pstart + counts, pend[-1:]]).astype(jnp.int32)
    fill_hi = jnp.concatenate([pend, tail]).astype(jnp.int32)
    return dest.astype(jnp.int32), block_expert, block_valid, fill_lo, fill_hi


def _tiles(b, L):
    t = b * L
    return dict(
        tm_proj=min(L, 512),
        lc=min(L, 64),
        tq=min(L, 128),
        tm_out=min(L, 256),
        tm_moe=min(L, 128),
        tm_disp=min(t, 512),
        rb=256 if t >= 8192 else 64,
    )


def _run_layer(l, x, mod, k_past, v_past, s0_re, s0_im, prm):
    b, L, _ = x.shape
    t = b * L
    tl = _tiles(b, L)
    sh1, sc1, g1, sh2, sc2, g2 = [m[:, None, :] for m in jnp.split(mod, 6, axis=-1)]

    u, q, k, v = in_projection(l, x, sc1, sh1, prm['w_in'], tl['tm_proj'])
    s5_out, s_re, s_im = s5_mixer(l, u, s0_re, s0_im, prm['wb'], prm['ab_re'], prm['ab_im'], prm['wc'],
                                  prm['s5_d'], prm['w_glu'], prm['b_glu'], tl['lc'])
    if k_past is None:
        attn = stick_breaking(q, k, v, tl['tq'], 128, 0)
    else:
        past = k_past.shape[2]
        tk = 128
        total = -(-(past + L) // tk) * tk
        pad = jnp.zeros((b, SB_HEADS, total - past - L, SB_HEAD_DIM), F32)
        k_all = jnp.concatenate([k_past.astype(F32), k, pad], axis=2)
        v_all = jnp.concatenate([v_past.astype(F32), v, pad], axis=2)
        attn = stick_breaking(q, k_all, v_all, tl['tq'], tk, past)

    x1, h2_tiles, top_idx, gates, rank, counts = out_projection(
        l, s5_out, attn, x, g1, sc2, sh2, prm['w_out'], prm['ln1_g'], prm['ln1_b'],
        prm['w_router'], prm['b_router'], tl['tm_out'])

    rb = tl['rb']
    n_assign = t * TOP_K
    n_blocks = (n_assign + N_EXPERTS * (rb - 1) + rb - 1) // rb
    dest, block_expert, block_valid, fill_lo, fill_hi = _routing_tables(
        counts[0, :N_EXPERTS], top_idx[:, :TOP_K], rank[:, :TOP_K], rb, n_blocks)
    xs_tiles = moe_dispatch(fill_lo, fill_hi, dest, h2_tiles, n_blocks * rb, tl['tm_disp'])
    ys_tiles = moe_experts(l, block_expert, block_valid, xs_tiles, prm['w_gate_up'], prm['b_gate_up'],
                           prm['w_down'], prm['b_down'], rb)
    x2 = moe_combine(dest, ys_tiles, gates, x1, g2, prm['ln2_g'], prm['ln2_b'], tl['tm_moe'])

    hshape = (b, S5_GROUPS, S5_STATE)
    return x2, k, v, s_re.reshape(hshape), s_im.reshape(hshape)


def kernel(x_prompt, x_sample, cache_k, cache_v, state_s5_re, state_s5_im, c_prompt, c_sample, ln_in_g, ln_in_b, w_ada, b_ada, w_in, s5_a_re, s5_a_im, s5_log_dt, s5_b_re, s5_b_im, s5_c_re, s5_c_im, s5_d, s5_w_glu, s5_b_glu, w_out, ln1_g, ln1_b, w_router, b_router, w_gate_up, b_gate_up, w_down, b_down, ln2_g, ln2_b):
    bp, lp, _ = x_prompt.shape
    bs, ls, _ = x_sample.shape

    mod_all = adaln_all(jnp.concatenate([c_prompt, c_sample], axis=0), w_ada, b_ada)
    xp = ln_rows(x_prompt.reshape(bp * lp, D_MODEL), ln_in_g, ln_in_b, min(512, bp * lp)).reshape(bp, lp, D_MODEL)
    xs = ln_rows(x_sample.reshape(bs * ls, D_MODEL), ln_in_g, ln_in_b, min(512, bs * ls)).reshape(bs, ls, D_MODEL)

    w_in_bf = w_in.astype(BF16)
    w_out_bf = w_out.astype(BF16)
    w_glu_bf = s5_w_glu.astype(BF16)
    wgu_bf = w_gate_up.astype(BF16)
    wd_bf = w_down.astype(BF16)
    w_router_pad = jnp.pad(w_router.astype(F32), ((0, 0), (0, 0), (0, LANES - N_EXPERTS)))
    b_router_pad = jnp.pad(b_router.astype(F32), ((0, 0), (0, LANES - N_EXPERTS)), constant_values=-1e30)
    zeros_p = jnp.zeros((bp, S5_NSTATE), F32)

    kp, vp, srp, sip = [], [], [], []
    ksm, vsm, srs, sis = [], [], [], []
    for li in range(DEPTH):
        l = jnp.full((1,), li, jnp.int32)
        ab_re, ab_im, wb, wc = s5_discretise(s5_a_re[li], s5_a_im[li], s5_log_dt[li], s5_b_re[li], s5_b_im[li],
                                             s5_c_re[li], s5_c_im[li])
        prm = dict(w_in=w_in_bf, w_out=w_out_bf, w_glu=w_glu_bf, w_gate_up=wgu_bf, w_down=wd_bf,
                   b_gate_up=b_gate_up, b_down=b_down,
                   ab_re=ab_re, ab_im=ab_im, wb=wb, wc=wc,
                   s5_d=s5_d[li].reshape(1, S5_WIDTH), b_glu=s5_b_glu[li].reshape(1, S5_WIDTH),
                   ln1_g=ln1_g[li].reshape(1, D_MODEL), ln1_b=ln1_b[li].reshape(1, D_MODEL),
                   ln2_g=ln2_g[li].reshape(1, D_MODEL), ln2_b=ln2_b[li].reshape(1, D_MODEL),
                   w_router=w_router_pad[li], b_router=b_router_pad[li].reshape(1, LANES))
        xp, k, v, s_re, s_im = _run_layer(l, xp, mod_all[li, :bp], None, None, zeros_p, zeros_p, prm)
        kp.append(k); vp.append(v); srp.append(s_re); sip.append(s_im)
        xs, k, v, s_re, s_im = _run_layer(l, xs, mod_all[li, bp:], cache_k[li], cache_v[li],
                                          state_s5_re[li].reshape(bs, S5_NSTATE),
                                          state_s5_im[li].reshape(bs, S5_NSTATE), prm)
        ksm.append(k); vsm.append(v); srs.append(s_re); sis.append(s_im)
    return (xp, xs, jnp.stack(kp), jnp.stack(vp), jnp.stack(srp), jnp.stack(sip),
            jnp.stack(ksm), jnp.stack(vsm), jnp.stack(srs), jnp.stack(sis))
```

```python
import functools
import math

import jax
import jax.numpy as jnp
from jax import lax
from jax.experimental import pallas as pl
from jax.experimental.pallas import tpu as pltpu

F32 = jnp.float32
BF16 = jnp.bfloat16
HIGHEST = lax.Precision.HIGHEST

D_MODEL = 1024
DEPTH = 4
S5_WIDTH = D_MODEL // 2
S5_GROUP_CH = 16
S5_GROUPS = S5_WIDTH // S5_GROUP_CH
S5_STATE = 64
S5_NSTATE = S5_GROUPS * S5_STATE
SB_WIDTH = D_MODEL - S5_WIDTH
SB_HEAD_DIM = 64
SB_HEADS = SB_WIDTH // SB_HEAD_DIM
N_EXPERTS = 32
TOP_K = 4
D_FF = D_MODEL
SWIGLU_LIMIT = 7.0
SWIGLU_ALPHA = 1.702
LN_EPS = 1e-5
DEEPNORM_ALPHA = (2 * DEPTH) ** 0.25

LANES = 128
SUBLANES = 8
ROW_TILES = D_MODEL // LANES
VMEM_LIMIT = 56 * 1024 * 1024


def _cparams(n_axes):
    return pltpu.CompilerParams(dimension_semantics=("arbitrary",) * n_axes,
                                vmem_limit_bytes=VMEM_LIMIT)


def _layer_norm(x, g, b):
    mu = jnp.mean(x, axis=-1, keepdims=True)
    xc = x - mu
    var = jnp.mean(xc * xc, axis=-1, keepdims=True)
    return xc * lax.rsqrt(var + LN_EPS) * g + b


def _ada_kernel(c_ref, w_ref, b_ref, o_ref):
    c = c_ref[...]
    s = c * jax.nn.sigmoid(c)
    o_ref[0] = jnp.dot(s, w_ref[0], precision=HIGHEST, preferred_element_type=F32) + b_ref[0]


def adaln_all(c_all, w_ada, b_ada):
    bc = c_all.shape[0]
    return pl.pallas_call(
        _ada_kernel,
        grid=(DEPTH, 6),
        in_specs=[pl.BlockSpec((bc, D_MODEL), lambda l, j: (0, 0)),
                  pl.BlockSpec((1, D_MODEL, D_MODEL), lambda l, j: (l, 0, j)),
                  pl.BlockSpec((1, 1, D_MODEL), lambda l, j: (l, 0, j))],
        out_specs=pl.BlockSpec((1, bc, D_MODEL), lambda l, j: (l, 0, j)),
        out_shape=jax.ShapeDtypeStruct((DEPTH, bc, 6 * D_MODEL), F32),
        compiler_params=_cparams(2),
        name="adaln",
    )(c_all, w_ada, b_ada.reshape(DEPTH, 1, 6 * D_MODEL))


def _ln_kernel(x_ref, g_ref, b_ref, o_ref):
    o_ref[...] = _layer_norm(x_ref[...], g_ref[...], b_ref[...])


def ln_rows(x2d, g, b, tm):
    t = x2d.shape[0]
    return pl.pallas_call(
        _ln_kernel,
        grid=(t // tm,),
        in_specs=[pl.BlockSpec((tm, D_MODEL), lambda i: (i, 0)),
                  pl.BlockSpec((1, D_MODEL), lambda i: (0, 0)),
                  pl.BlockSpec((1, D_MODEL), lambda i: (0, 0))],
        out_specs=pl.BlockSpec((tm, D_MODEL), lambda i: (i, 0)),
        out_shape=jax.ShapeDtypeStruct((t, D_MODEL), F32),
        compiler_params=_cparams(1),
        name="ln_in",
    )(x2d, g.reshape(1, D_MODEL), b.reshape(1, D_MODEL))


def _inproj_kernel(l_ref, x_ref, sc_ref, sh_ref, w_ref, u_ref, q_ref, k_ref, v_ref):
    del l_ref
    h = x_ref[0] * (1.0 + sc_ref[0]) + sh_ref[0]
    proj = jnp.dot(h.astype(BF16), w_ref[0], preferred_element_type=F32)
    u_ref[0] = proj[:, :S5_WIDTH]
    for hh in range(SB_HEADS):
        lo = S5_WIDTH + hh * SB_HEAD_DIM
        q_ref[0, hh] = proj[:, lo:lo + SB_HEAD_DIM]
        k_ref[0, hh] = proj[:, lo + SB_WIDTH:lo + SB_WIDTH + SB_HEAD_DIM]
        v_ref[0, hh] = proj[:, lo + 2 * SB_WIDTH:lo + 2 * SB_WIDTH + SB_HEAD_DIM]


def in_projection(l, x, sc1, sh1, w_in_bf16, tm):
    b, L, _ = x.shape
    n_proj = S5_WIDTH + 3 * SB_WIDTH
    mod_spec = pl.BlockSpec((1, 1, D_MODEL), lambda i, j, l_ref: (i, 0, 0))
    head_spec = pl.BlockSpec((1, SB_HEADS, tm, SB_HEAD_DIM), lambda i, j, l_ref: (i, 0, j, 0))
    head_shape = jax.ShapeDtypeStruct((b, SB_HEADS, L, SB_HEAD_DIM), F32)
    return pl.pallas_call(
        _inproj_kernel,
        grid_spec=pltpu.PrefetchScalarGridSpec(
            num_scalar_prefetch=1,
            grid=(b, L // tm),
            in_specs=[pl.BlockSpec((1, tm, D_MODEL), lambda i, j, l_ref: (i, j, 0)),
                      mod_spec, mod_spec,
                      pl.BlockSpec((1, D_MODEL, n_proj), lambda i, j, l_ref: (l_ref[0], 0, 0))],
            out_specs=[pl.BlockSpec((1, tm, S5_WIDTH), lambda i, j, l_ref: (i, j, 0)),
                       head_spec, head_spec, head_spec]),
        out_shape=[jax.ShapeDtypeStruct((b, L, S5_WIDTH), F32), head_shape, head_shape, head_shape],
        compiler_params=_cparams(2),
        name="in_proj",
    )(l, x, sc1, sh1, w_in_bf16)


S5_BATCH = SUBLANES
S5_SLABS = S5_NSTATE // LANES
S5_CHUNKS = S5_WIDTH // LANES
S5_CHUNK_STATES = S5_NSTATE // S5_CHUNKS
S5_CHUNK_SLABS = S5_CHUNK_STATES // LANES


def _s5_kernel(l_ref, u_ref, x0r_ref, x0i_ref, wb_ref, ar_ref, ai_ref, wc_ref, d_ref, wg_ref, bg_ref,
               o_ref, sr_ref, si_ref, upad, slab, st_re, st_im, *, lc, pitch):
    del l_ref
    ci = pl.program_id(1)

    @pl.when(ci == 0)
    def _():
        upad[...] = jnp.zeros_like(upad)
        st_re[...] = x0r_ref[...]
        st_im[...] = x0i_ref[...]

    for b in range(S5_BATCH):
        upad[b * pitch:b * pitch + lc, :] = u_ref[b]

    for c in range(S5_CHUNKS):
        r = jnp.dot(upad[:, c * LANES:(c + 1) * LANES].astype(BF16), wb_ref[c],
                    preferred_element_type=F32)
        for j in range(S5_CHUNK_SLABS):
            slab[c * S5_CHUNK_SLABS + j] = r[:, j * LANES:(j + 1) * LANES]
            slab[S5_SLABS + c * S5_CHUNK_SLABS + j] = r[:, S5_CHUNK_STATES + j * LANES:
                                                          S5_CHUNK_STATES + (j + 1) * LANES]

    for c in range(S5_CHUNKS):
        lane0 = c * S5_CHUNK_STATES
        ar = [jnp.broadcast_to(ar_ref[:, lane0 + j * LANES:lane0 + (j + 1) * LANES], (S5_BATCH, LANES))
              for j in range(S5_CHUNK_SLABS)]
        ai = [jnp.broadcast_to(ai_ref[:, lane0 + j * LANES:lane0 + (j + 1) * LANES], (S5_BATCH, LANES))
              for j in range(S5_CHUNK_SLABS)]
        xr0 = tuple(st_re[:, lane0 + j * LANES:lane0 + (j + 1) * LANES] for j in range(S5_CHUNK_SLABS))
        xi0 = tuple(st_im[:, lane0 + j * LANES:lane0 + (j + 1) * LANES] for j in range(S5_CHUNK_SLABS))

        def step(t, carry, c=c, ar=ar, ai=ai):
            xr, xi = carry
            nxr, nxi = [], []
            for j in range(S5_CHUNK_SLABS):
                s_re = c * S5_CHUNK_SLABS + j
                s_im = S5_SLABS + s_re
                rows_t = pl.ds(t, S5_BATCH, stride=pitch)
                nr = ar[j] * xr[j] - ai[j] * xi[j] + slab[s_re, rows_t, :]
                ni = ar[j] * xi[j] + ai[j] * xr[j] + slab[s_im, rows_t, :]
                slab[s_re, rows_t, :] = nr
                slab[s_im, rows_t, :] = ni
                nxr.append(nr)
                nxi.append(ni)
            return tuple(nxr), tuple(nxi)

        xr, xi = lax.fori_loop(0, lc, step, (xr0, xi0))
        for j in range(S5_CHUNK_SLABS):
            st_re[:, lane0 + j * LANES:lane0 + (j + 1) * LANES] = xr[j]
            st_im[:, lane0 + j * LANES:lane0 + (j + 1) * LANES] = xi[j]

    sr_ref[...] = st_re[...]
    si_ref[...] = st_im[...]

    ys = []
    for c in range(S5_CHUNKS):
        xs = jnp.concatenate(
            [slab[c * S5_CHUNK_SLABS + j] for j in range(S5_CHUNK_SLABS)]
            + [slab[S5_SLABS + c * S5_CHUNK_SLABS + j] for j in range(S5_CHUNK_SLABS)], axis=1)
        ys.append(jnp.dot(xs.astype(BF16), wc_ref[c], preferred_element_type=F32))
    y = jnp.concatenate(ys, axis=1) + d_ref[...] * upad[...]
    g = jax.nn.gelu(y)
    gate = jnp.dot(g.astype(BF16), wg_ref[0], preferred_element_type=F32) + bg_ref[...]
    out = g * jax.nn.sigmoid(gate)
    for b in range(S5_BATCH):
        o_ref[b] = out[b * pitch:b * pitch + lc, :]


def s5_mixer(l, u, x0_re, x0_im, wb, ab_re, ab_im, wc, d, w_glu_bf16, b_glu, lc):
    b, L, _ = u.shape
    pitch = lc + SUBLANES
    rows = S5_BATCH * pitch
    kern = functools.partial(_s5_kernel, lc=lc, pitch=pitch)
    const2 = lambda i, j, l_ref: (0, 0)
    const3 = lambda i, j, l_ref: (0, 0, 0)
    state_spec = pl.BlockSpec((S5_BATCH, S5_NSTATE), lambda i, j, l_ref: (i, 0))
    return pl.pallas_call(
        kern,
        grid_spec=pltpu.PrefetchScalarGridSpec(
            num_scalar_prefetch=1,
            grid=(b // S5_BATCH, L // lc),
            in_specs=[pl.BlockSpec((S5_BATCH, lc, S5_WIDTH), lambda i, j, l_ref: (i, j, 0)),
                      state_spec, state_spec,
                      pl.BlockSpec((S5_CHUNKS, LANES, 2 * S5_CHUNK_STATES), const3),
                      pl.BlockSpec((1, S5_NSTATE), const2),
                      pl.BlockSpec((1, S5_NSTATE), const2),
                      pl.BlockSpec((S5_CHUNKS, 2 * S5_CHUNK_STATES, LANES), const3),
                      pl.BlockSpec((1, S5_WIDTH), const2),
                      pl.BlockSpec((1, S5_WIDTH, S5_WIDTH), lambda i, j, l_ref: (l_ref[0], 0, 0)),
                      pl.BlockSpec((1, S5_WIDTH), const2)],
            out_specs=[pl.BlockSpec((S5_BATCH, lc, S5_WIDTH), lambda i, j, l_ref: (i, j, 0)),
                       state_spec, state_spec],
            scratch_shapes=[pltpu.VMEM((rows, S5_WIDTH), F32),
                            pltpu.VMEM((2 * S5_SLABS, rows, LANES), F32),
                            pltpu.VMEM((S5_BATCH, S5_NSTATE), F32),
                            pltpu.VMEM((S5_BATCH, S5_NSTATE), F32)]),
        out_shape=[jax.ShapeDtypeStruct((b, L, S5_WIDTH), F32),
                   jax.ShapeDtypeStruct((b, S5_NSTATE), F32),
                   jax.ShapeDtypeStruct((b, S5_NSTATE), F32)],
        compiler_params=_cparams(2),
        name="s5_mixer",
    )(l, u, x0_re, x0_im, wb, ab_re, ab_im, wc, d, w_glu_bf16, b_glu)


def s5_discretise(a_re, a_im, log_dt, b_re, b_im, c_re, c_im):
    a_re, a_im = a_re.astype(F32), a_im.astype(F32)
    dt = jnp.exp(log_dt.astype(F32))[:, None]
    mag = jnp.exp(a_re * dt)
    ab_re = mag * jnp.cos(a_im * dt)
    ab_im = mag * jnp.sin(a_im * dt)
    den = a_re * a_re + a_im * a_im
    nr, ni = ab_re - 1.0, ab_im
    f_re = (nr * a_re + ni * a_im) / den
    f_im = (ni * a_re - nr * a_im) / den
    b_re, b_im = b_re.astype(F32), b_im.astype(F32)
    bb_re = f_re[..., None] * b_re - f_im[..., None] * b_im
    bb_im = f_re[..., None] * b_im + f_im[..., None] * b_re
    gpc = S5_GROUPS // S5_CHUNKS
    eye = jnp.eye(gpc, dtype=F32)

    def blockdiag_in(bb):
        bb = bb.reshape(S5_CHUNKS, gpc, S5_STATE, S5_GROUP_CH)
        m = jnp.einsum('cgpn,gh->cgnhp', bb, eye)
        return m.reshape(S5_CHUNKS, gpc * S5_GROUP_CH, gpc * S5_STATE)

    def blockdiag_out(cc):
        cc = cc.reshape(S5_CHUNKS, gpc, S5_GROUP_CH, S5_STATE)
        m = jnp.einsum('cgnp,gh->cgphn', cc, eye)
        return m.reshape(S5_CHUNKS, gpc * S5_STATE, gpc * S5_GROUP_CH)

    wb = jnp.concatenate([blockdiag_in(bb_re), blockdiag_in(bb_im)], axis=2).astype(BF16)
    wc = jnp.concatenate([blockdiag_out(c_re.astype(F32)), -blockdiag_out(c_im.astype(F32))],
                         axis=1).astype(BF16)
    return ab_re.reshape(1, S5_NSTATE), ab_im.reshape(1, S5_NSTATE), wb, wc


ATTN_HEADS_PER_STEP = LANES // SB_HEAD_DIM
ATTN_KEY_BLOCK = LANES
UNDERFLOW_LOG = -104.0


def _suffix_and_total(width, total_lanes):
    shape = (2 * width, width + total_lanes)
    s_idx = lax.broadcasted_iota(jnp.int32, shape, 0) % width
    j_idx = lax.broadcasted_iota(jnp.int32, shape, 1)
    return jnp.where((s_idx > j_idx) | (j_idx >= width), 1.0, 0.0).astype(BF16)


def _attn_kernel(l_ref, q_ref, ks_ref, vs_ref, kp_ref, vp_ref, o_ref, *, tq, tk, n_past):
    del l_ref
    jd = pl.program_id(2) * (tq // tk) if n_past is None else n_past
    scale = SB_HEAD_DIM ** -0.5
    heads = range(ATTN_HEADS_PER_STEP)
    qs = [(q_ref[0, hh] * scale).astype(BF16) for hh in heads]

    def scores(hh, kb):
        return lax.dot_general(qs[hh], kb.astype(BF16), (((1,), (1,)), ((), ())), preferred_element_type=F32)

    def past_keys(hh, j):
        return kp_ref[0, 0, hh, pl.ds(pl.multiple_of(j * tk, tk), tk), :]

    def past_values(hh, j):
        return vp_ref[0, 0, hh, pl.ds(pl.multiple_of(j * tk, tk), tk), :].astype(BF16)

    def block_terms(z, mask, sums, width):
        sp = jnp.log(1.0 + jnp.exp(-jnp.abs(z)))
        log_not = -(jnp.maximum(z, 0.0) + sp)
        log_beta = jnp.minimum(z, 0.0) - sp
        if mask is not None:
            log_not = jnp.where(mask, log_not, 0.0)
        hi = log_not.astype(BF16)
        lo = (log_not - hi.astype(F32)).astype(BF16)
        cs = jnp.dot(jnp.concatenate([hi, lo], axis=1), sums, preferred_element_type=F32)
        return log_beta, cs[:, :width], cs[:, width:]

    mask = lax.broadcasted_iota(jnp.int32, (tq, tq), 1) < lax.broadcasted_iota(jnp.int32, (tq, tq), 0)
    own_sums = _suffix_and_total(tq, tk)
    carries, accs = [], []
    for hh in heads:
        log_beta, later, total = block_terms(scores(hh, ks_ref[0, hh]), mask, own_sums, tq)
        w = jnp.where(mask, jnp.exp(log_beta + later), 0.0)
        accs.append(jnp.dot(w.astype(BF16), vs_ref[0, hh].astype(BF16), preferred_element_type=F32))
        carries.append(total)
    past_sums = _suffix_and_total(tk, tk)

    def live(carries):
        return jnp.max(jnp.maximum(carries[0], carries[1])) > UNDERFLOW_LOG

    def block_pair(it, carries, accs):
        ja_raw = jd - 1 - 2 * it
        has_a, has_b = ja_raw >= 0, ja_raw >= 1
        ja, jb = jnp.maximum(ja_raw, 0), jnp.maximum(ja_raw - 1, 0)
        new_carries, new_accs = [], []
        for hh in heads:
            lb_a, later_a, total_a = block_terms(scores(hh, past_keys(hh, ja)), None, past_sums, tk)
            lb_b, later_b, total_b = block_terms(scores(hh, past_keys(hh, jb)), None, past_sums, tk)
            carry_b = carries[hh] + jnp.where(has_a, total_a, 0.0)
            w_a = jnp.exp(lb_a + later_a + carries[hh]).astype(BF16)
            w_b = jnp.exp(lb_b + later_b + carry_b).astype(BF16)
            w_a = jnp.where(has_a, w_a, jnp.zeros_like(w_a))
            w_b = jnp.where(has_b, w_b, jnp.zeros_like(w_b))
            new_accs.append(accs[hh]
                            + jnp.dot(w_a, past_values(hh, ja), preferred_element_type=F32)
                            + jnp.dot(w_b, past_values(hh, jb), preferred_element_type=F32))
            new_carries.append(carry_b + jnp.where(has_b, total_b, 0.0))
        return tuple(new_carries), tuple(new_accs)

    def cond(state):
        it, _, _, go = state
        return (it < (jd + 1) // 2) & go

    def body(state):
        it, carries, accs, _ = state
        carries, accs = block_pair(it, carries, accs)
        return it + 1, carries, accs, live(carries)

    carries, accs = block_pair(0, tuple(carries), tuple(accs))
    _, _, accs, _ = lax.while_loop(cond, body, (jnp.int32(1), carries, accs, live(carries)))
    o_ref[0] = jnp.concatenate(accs, axis=1)


def stick_breaking(l, q, k, v, k_past, v_past, tq, tk):
    b, h, L, dh = q.shape
    hp = ATTN_HEADS_PER_STEP
    if k_past is None:
        assert tq % tk == 0
        k_past, v_past, n_past = k.reshape(1, b, h, L, dh), v.reshape(1, b, h, L, dh), None
        past_map = lambda i, j, t, l_ref: (0, i, j, 0, 0)
    else:
        assert L == tq and k_past.shape[3] % tk == 0
        n_past = k_past.shape[3] // tk
        past_map = lambda i, j, t, l_ref: (l_ref[0], i, j, 0, 0)
    kern = functools.partial(_attn_kernel, tq=tq, tk=tk, n_past=n_past)
    own_spec = pl.BlockSpec((1, hp, tq, dh), lambda i, j, t, l_ref: (i, j, t, 0))
    past_spec = pl.BlockSpec((1, 1, hp, k_past.shape[3], dh), past_map)
    return pl.pallas_call(
        kern,
        grid_spec=pltpu.PrefetchScalarGridSpec(
            num_scalar_prefetch=1,
            grid=(b, h // hp, L // tq),
            in_specs=[own_spec, own_spec, own_spec, past_spec, past_spec],
            out_specs=pl.BlockSpec((1, tq, hp * dh), lambda i, j, t, l_ref: (i, t, j))),
        out_shape=jax.ShapeDtypeStruct((b, L, h * dh), F32),
        compiler_params=_cparams(3),
        name="stick_breaking",
    )(l, q, k, v, k_past, v_past)


def _lane_pack(cols, tm):
    lane = lax.broadcasted_iota(jnp.int32, (tm, LANES), 1)
    out = jnp.zeros((tm, LANES), cols[0].dtype)
    for kk, col in enumerate(cols):
        out = jnp.where(lane == kk, col, out)
    return out


def _outproj_kernel(l_ref, s5_ref, at_ref, x_ref, g1_ref, sc2_ref, sh2_ref, w_ref, lg_ref, lb_ref,
                    wr_ref, br_ref, x1_ref, h2_ref, idx_ref, gate_ref, rank_ref, cnt_ref, counts, *, tm):
    del l_ref
    first = (pl.program_id(0) == 0) & (pl.program_id(1) == 0)

    @pl.when(first)
    def _():
        counts[...] = jnp.zeros_like(counts)

    mix = (jnp.dot(s5_ref[0].astype(BF16), w_ref[0, :S5_WIDTH, :], preferred_element_type=F32)
           + jnp.dot(at_ref[0].astype(BF16), w_ref[0, S5_WIDTH:, :], preferred_element_type=F32))
    x1 = _layer_norm(DEEPNORM_ALPHA * x_ref[0] + (1.0 + g1_ref[0]) * mix, lg_ref[...], lb_ref[...])
    x1_ref[0] = x1
    h2 = x1 * (1.0 + sc2_ref[0]) + sh2_ref[0]
    for s in range(ROW_TILES):
        h2_ref[pl.ds(s, tm, stride=ROW_TILES), :] = h2[:, s * LANES:(s + 1) * LANES]

    logits = jnp.dot(h2, wr_ref[...], precision=HIGHEST, preferred_element_type=F32) + br_ref[...]
    lane = lax.broadcasted_iota(jnp.int32, (tm, LANES), 1)
    work = logits
    vals, idxs, hots = [], [], []
    for _ in range(TOP_K):
        m = jnp.max(work, axis=1, keepdims=True)
        sel = jnp.min(jnp.where(work == m, lane, LANES), axis=1, keepdims=True)
        hot = lane == sel
        work = jnp.where(hot, -jnp.inf, work)
        vals.append(m)
        idxs.append(sel)
        hots.append(hot)
    exps = [jnp.exp(vv - vals[0]) for vv in vals]
    denom = exps[0] + exps[1] + exps[2] + exps[3]
    gates = [e / denom for e in exps]

    hot_any = (hots[0] | hots[1] | hots[2] | hots[3]).astype(F32)
    r_idx = lax.broadcasted_iota(jnp.int32, (tm, tm), 0)
    c_idx = lax.broadcasted_iota(jnp.int32, (tm, tm), 1)
    earlier = jnp.where(c_idx < r_idx, 1.0, 0.0).astype(BF16)
    prefix = jnp.dot(earlier, hot_any.astype(BF16), preferred_element_type=F32) + counts[0:1, :]
    ranks = [jnp.sum(jnp.where(hot, prefix, 0.0), axis=1, keepdims=True).astype(jnp.int32) for hot in hots]
    counts[...] = counts[...] + jnp.sum(hot_any, axis=0, keepdims=True)

    idx_ref[...] = _lane_pack(idxs, tm)
    gate_ref[...] = _lane_pack(gates, tm)
    rank_ref[...] = _lane_pack(ranks, tm)
    cnt_ref[...] = counts[...]


def out_projection(l, s5_out, attn, x, g1, sc2, sh2, w_out_bf16, ln_g, ln_b, w_router_pad, b_router_pad, tm):
    b, L, _ = x.shape
    t = b * L
    nj = L // tm
    kern = functools.partial(_outproj_kernel, tm=tm)
    mod_spec = pl.BlockSpec((1, 1, D_MODEL), lambda i, j, l_ref: (i, 0, 0))
    vec_spec = pl.BlockSpec((1, D_MODEL), lambda i, j, l_ref: (0, 0))
    half_spec = pl.BlockSpec((1, tm, S5_WIDTH), lambda i, j, l_ref: (i, j, 0))
    full_spec = pl.BlockSpec((1, tm, D_MODEL), lambda i, j, l_ref: (i, j, 0))
    tok_spec = pl.BlockSpec((tm, LANES), lambda i, j, l_ref: (i * nj + j, 0))
    return pl.pallas_call(
        kern,
        grid_spec=pltpu.PrefetchScalarGridSpec(
            num_scalar_prefetch=1,
            grid=(b, nj),
            in_specs=[half_spec, half_spec, full_spec, mod_spec, mod_spec, mod_spec,
                      pl.BlockSpec((1, D_MODEL, D_MODEL), lambda i, j, l_ref: (l_ref[0], 0, 0)),
                      vec_spec, vec_spec,
                      pl.BlockSpec((D_MODEL, LANES), lambda i, j, l_ref: (0, 0)),
                      pl.BlockSpec((1, LANES), lambda i, j, l_ref: (0, 0))],
            out_specs=[full_spec,
                       pl.BlockSpec((tm * ROW_TILES, LANES), lambda i, j, l_ref: (i * nj + j, 0)),
                       tok_spec, tok_spec, tok_spec,
                       pl.BlockSpec((SUBLANES, LANES), lambda i, j, l_ref: (0, 0))],
            scratch_shapes=[pltpu.VMEM((SUBLANES, LANES), F32)]),
        out_shape=[jax.ShapeDtypeStruct((b, L, D_MODEL), F32),
                   jax.ShapeDtypeStruct((t * ROW_TILES, LANES), F32),
                   jax.ShapeDtypeStruct((t, LANES), jnp.int32),
                   jax.ShapeDtypeStruct((t, LANES), F32),
                   jax.ShapeDtypeStruct((t, LANES), jnp.int32),
                   jax.ShapeDtypeStruct((SUBLANES, LANES), F32)],
        compiler_params=_cparams(2),
        name="out_proj_router",
    )(l, s5_out, attn, x, g1, sc2, sh2, w_out_bf16, ln_g, ln_b, w_router_pad, b_router_pad)


def _row_copy(src, src_row, dst, dst_row, sem):
    return pltpu.make_async_copy(
        src.at[pl.ds(pl.multiple_of(src_row * ROW_TILES, ROW_TILES), ROW_TILES)],
        dst.at[pl.ds(pl.multiple_of(dst_row * ROW_TILES, ROW_TILES), ROW_TILES)], sem)


def _rows_wait(big_ref, n_rows, sem):
    span = big_ref.at[pl.ds(0, n_rows * ROW_TILES)]
    pltpu.make_async_copy(span, span, sem).wait()


def _dispatch_kernel(lo_ref, hi_ref, dest_ref, h2_ref, xs_ref, zero_tile, sem, fill_sem, *, tm, n_fill):
    n = tm * TOP_K

    @pl.when(pl.program_id(0) == 0)
    def _():
        zero_tile[...] = jnp.zeros_like(zero_tile)

        def fill_range(e, _):
            def fill_row(r, _):
                _row_copy(zero_tile, 0, xs_ref, r, fill_sem).start()
                return 0
            lax.fori_loop(lo_ref[e], hi_ref[e], fill_row, 0)
            return 0

        lax.fori_loop(0, N_EXPERTS + 1, fill_range, 0)
        _rows_wait(xs_ref, n_fill, fill_sem)

    def issue(t, _):
        for kk in range(TOP_K):
            _row_copy(h2_ref, t, xs_ref, dest_ref[t * TOP_K + kk], sem).start()
        return 0

    lax.fori_loop(0, tm, issue, 0, unroll=2)
    _rows_wait(xs_ref, n, sem)


def moe_dispatch(fill_lo, fill_hi, dest_flat, h2_tiles, n_rows, tm):
    n_assign = dest_flat.shape[0]
    kern = functools.partial(_dispatch_kernel, tm=tm, n_fill=n_rows - n_assign)
    return pl.pallas_call(
        kern,
        grid_spec=pltpu.PrefetchScalarGridSpec(
            num_scalar_prefetch=2,
            grid=(n_assign // (tm * TOP_K),),
            in_specs=[pl.BlockSpec((tm * TOP_K,), lambda i, lo, hi: (i,), memory_space=pltpu.SMEM),
                      pl.BlockSpec((tm * ROW_TILES, LANES), lambda i, lo, hi: (i, 0))],
            out_specs=pl.BlockSpec(memory_space=pl.ANY),
            scratch_shapes=[pltpu.VMEM((ROW_TILES, LANES), F32),
                            pltpu.SemaphoreType.DMA(()), pltpu.SemaphoreType.DMA(())]),
        out_shape=jax.ShapeDtypeStruct((n_rows * ROW_TILES, LANES), F32),
        compiler_params=_cparams(1),
        name="moe_dispatch",
    )(fill_lo, fill_hi, dest_flat, h2_tiles)


def _expert_kernel(l_ref, be_ref, nv_ref, xs_ref, wgu_ref, bgu_ref, wd_ref, bd_ref, ys_ref, *, rb):
    del l_ref, be_ref
    nvalid = nv_ref[pl.program_id(0)]

    @pl.when(nvalid > 0)
    def _():
        x = jnp.concatenate([xs_ref[pl.ds(s, rb, stride=ROW_TILES), :] for s in range(ROW_TILES)], axis=1)
        gu = jnp.dot(x.astype(BF16), wgu_ref[0, 0], preferred_element_type=F32) + bgu_ref[0, 0]
        gate = jnp.minimum(gu[:, :D_FF], SWIGLU_LIMIT)
        up = jnp.clip(gu[:, D_FF:], -SWIGLU_LIMIT, SWIGLU_LIMIT)
        act = (up + 1.0) * gate * jax.nn.sigmoid(SWIGLU_ALPHA * gate)
        y = jnp.dot(act.astype(BF16), wd_ref[0, 0], preferred_element_type=F32) + bd_ref[0, 0]
        for s in range(ROW_TILES):
            ys_ref[pl.ds(s, rb, stride=ROW_TILES), :] = y[:, s * LANES:(s + 1) * LANES]

    @pl.when(nvalid <= 0)
    def _():
        ys_ref[...] = jnp.zeros_like(ys_ref)


def moe_experts(l, block_expert, block_valid, xs_tiles, wgu_bf16, b_gate_up, wd_bf16, b_down, rb):
    n_blocks = block_expert.shape[0]
    kern = functools.partial(_expert_kernel, rb=rb)
    row_spec = pl.BlockSpec((rb * ROW_TILES, LANES), lambda i, l_ref, be, nv: (i, 0))
    return pl.pallas_call(
        kern,
        grid_spec=pltpu.PrefetchScalarGridSpec(
            num_scalar_prefetch=3,
            grid=(n_blocks,),
            in_specs=[row_spec,
                      pl.BlockSpec((1, 1, D_MODEL, 2 * D_FF), lambda i, l_ref, be, nv: (l_ref[0], be[i], 0, 0)),
                      pl.BlockSpec((1, 1, 1, 2 * D_FF), lambda i, l_ref, be, nv: (l_ref[0], be[i], 0, 0)),
                      pl.BlockSpec((1, 1, D_FF, D_MODEL), lambda i, l_ref, be, nv: (l_ref[0], be[i], 0, 0)),
                      pl.BlockSpec((1, 1, 1, D_MODEL), lambda i, l_ref, be, nv: (l_ref[0], be[i], 0, 0))],
            out_specs=row_spec),
        out_shape=jax.ShapeDtypeStruct(xs_tiles.shape, F32),
        compiler_params=_cparams(1),
        name="moe_experts",
    )(l, block_expert, block_valid, xs_tiles, wgu_bf16,
      b_gate_up.reshape(DEPTH, N_EXPERTS, 1, 2 * D_FF), wd_bf16, b_down.reshape(DEPTH, N_EXPERTS, 1, D_MODEL))


def _combine_kernel(dest_ref, next_ref, ys_ref, gate_ref, x1_ref, g2_ref, lg_ref, lb_ref, o_ref, buf, sems,
                    *, tm, n_steps):
    n = tm * TOP_K
    step = pl.program_id(0) * pl.num_programs(1) + pl.program_id(1)
    half = step % 2

    def gather(table, into):
        def issue(t, _):
            for kk in range(TOP_K):
                _row_copy(ys_ref, table[t * TOP_K + kk], buf, into * n + kk * tm + t, sems.at[into]).start()
            return 0
        lax.fori_loop(0, tm, issue, 0, unroll=2)

    @pl.when(step == 0)
    def _():
        gather(dest_ref, half)

    @pl.when(step + 1 < n_steps)
    def _():
        gather(next_ref, 1 - half)

    _rows_wait(buf, n, sems.at[half])

    base = half * n * ROW_TILES
    ff = jnp.zeros((tm, D_MODEL), F32)
    for kk in range(TOP_K):
        rows = jnp.concatenate(
            [buf[pl.ds(base + kk * tm * ROW_TILES + s, tm, stride=ROW_TILES), :] for s in range(ROW_TILES)],
            axis=1)
        ff = ff + rows * gate_ref[:, kk:kk + 1]
    o_ref[0] = _layer_norm(DEEPNORM_ALPHA * x1_ref[0] + (1.0 + g2_ref[0]) * ff, lg_ref[...], lb_ref[...])


def moe_combine(dest_flat, ys_tiles, gates, x1, g2, ln_g, ln_b, tm):
    b, L, _ = x1.shape
    nj = L // tm
    n_steps = b * nj
    kern = functools.partial(_combine_kernel, tm=tm, n_steps=n_steps)
    full_spec = pl.BlockSpec((1, tm, D_MODEL), lambda i, j: (i, j, 0))
    vec_spec = pl.BlockSpec((1, D_MODEL), lambda i, j: (0, 0))
    return pl.pallas_call(
        kern,
        grid=(b, nj),
        in_specs=[pl.BlockSpec((tm * TOP_K,), lambda i, j: (i * nj + j,), memory_space=pltpu.SMEM),
                  pl.BlockSpec((tm * TOP_K,), lambda i, j: (jnp.minimum(i * nj + j + 1, n_steps - 1),),
                               memory_space=pltpu.SMEM),
                  pl.BlockSpec(memory_space=pl.ANY),
                  pl.BlockSpec((tm, LANES), lambda i, j: (i * nj + j, 0)),
                  full_spec,
                  pl.BlockSpec((1, 1, D_MODEL), lambda i, j: (i, 0, 0)),
                  vec_spec, vec_spec],
        out_specs=full_spec,
        out_shape=jax.ShapeDtypeStruct((b, L, D_MODEL), F32),
        scratch_shapes=[pltpu.VMEM((2 * tm * TOP_K * ROW_TILES, LANES), F32), pltpu.SemaphoreType.DMA((2,))],
        compiler_params=_cparams(2),
        name="moe_combine",
    )(dest_flat, dest_flat, ys_tiles, gates, x1, g2, ln_g, ln_b)


def _routing_tables(counts, top_idx, rank, rb, n_blocks):
    counts = counts.astype(jnp.int32)
    padded = (counts + rb - 1) // rb * rb
    pend = jnp.cumsum(padded)
    pstart = pend - padded
    dest = (pstart[top_idx] + rank).reshape(-1)
    row0 = jnp.arange(n_blocks, dtype=jnp.int32) * rb
    ends_passed = jnp.sum((pend[None, :] <= row0[:, None]).astype(jnp.int32), axis=1)
    block_expert = jnp.minimum(ends_passed, N_EXPERTS - 1).astype(jnp.int32)
    block_valid = jnp.clip(counts[block_expert] - (row0 - pstart[block_expert]), 0, rb).astype(jnp.int32)
    tail = jnp.full((1,), n_blocks * rb, jnp.int32)
    fill_lo = jnp.concatenate([pstart + counts, pend[-1:]]).astype(jnp.int32)
    fill_hi = jnp.concatenate([pend, tail]).astype(jnp.int32)
    return dest.astype(jnp.int32), block_expert, block_valid, fill_lo, fill_hi


def _tiles(b, L):
    t = b * L
    return dict(
        tm_proj=min(L, 512),
        lc=min(L, 64),
        tq=min(L, 128),
        tm_out=min(L, 256),
        tm_moe=min(L, 128),
        tm_disp=min(t, 512),
        rb=512 if t >= 8192 else 64,
    )


def _run_layer(l, x, mod, k_past, v_past, s0_re, s0_im, prm):
    b, L, _ = x.shape
    t = b * L
    tl = _tiles(b, L)
    sh1, sc1, g1, sh2, sc2, g2 = [m[:, None, :] for m in jnp.split(mod, 6, axis=-1)]

    u, q, k, v = in_projection(l, x, sc1, sh1, prm['w_in'], tl['tm_proj'])
    s5_out, s_re, s_im = s5_mixer(l, u, s0_re, s0_im, prm['wb'], prm['ab_re'], prm['ab_im'], prm['wc'],
                                  prm['s5_d'], prm['w_glu'], prm['b_glu'], tl['lc'])
    attn = stick_breaking(l, q, k, v, k_past, v_past, tl['tq'], ATTN_KEY_BLOCK)

    x1, h2_tiles, top_idx, gates, rank, counts = out_projection(
        l, s5_out, attn, x, g1, sc2, sh2, prm['w_out'], prm['ln1_g'], prm['ln1_b'],
        prm['w_router'], prm['b_router'], tl['tm_out'])

    rb = tl['rb']
    n_assign = t * TOP_K
    n_blocks = (n_assign + N_EXPERTS * (rb - 1) + rb - 1) // rb
    dest, block_expert, block_valid, fill_lo, fill_hi = _routing_tables(
        counts[0, :N_EXPERTS], top_idx[:, :TOP_K], rank[:, :TOP_K], rb, n_blocks)
    xs_tiles = moe_dispatch(fill_lo, fill_hi, dest, h2_tiles, n_blocks * rb, tl['tm_disp'])
    ys_tiles = moe_experts(l, block_expert, block_valid, xs_tiles, prm['w_gate_up'], prm['b_gate_up'],
                           prm['w_down'], prm['b_down'], rb)
    x2 = moe_combine(dest, ys_tiles, gates, x1, g2, prm['ln2_g'], prm['ln2_b'], tl['tm_moe'])

    hshape = (b, S5_GROUPS, S5_STATE)
    return x2, k, v, s_re.reshape(hshape), s_im.reshape(hshape)


def kernel(x_prompt, x_sample, cache_k, cache_v, state_s5_re, state_s5_im, c_prompt, c_sample, ln_in_g, ln_in_b, w_ada, b_ada, w_in, s5_a_re, s5_a_im, s5_log_dt, s5_b_re, s5_b_im, s5_c_re, s5_c_im, s5_d, s5_w_glu, s5_b_glu, w_out, ln1_g, ln1_b, w_router, b_router, w_gate_up, b_gate_up, w_down, b_down, ln2_g, ln2_b):
    bp, lp, _ = x_prompt.shape
    bs, ls, _ = x_sample.shape

    mod_all = adaln_all(jnp.concatenate([c_prompt, c_sample], axis=0), w_ada, b_ada)
    xp = ln_rows(x_prompt.reshape(bp * lp, D_MODEL), ln_in_g, ln_in_b, min(512, bp * lp)).reshape(bp, lp, D_MODEL)
    xs = ln_rows(x_sample.reshape(bs * ls, D_MODEL), ln_in_g, ln_in_b, min(512, bs * ls)).reshape(bs, ls, D_MODEL)

    w_in_bf = w_in.astype(BF16)
    w_out_bf = w_out.astype(BF16)
    w_glu_bf = s5_w_glu.astype(BF16)
    wgu_bf = w_gate_up.astype(BF16)
    wd_bf = w_down.astype(BF16)
    w_router_pad = jnp.pad(w_router.astype(F32), ((0, 0), (0, 0), (0, LANES - N_EXPERTS)))
    b_router_pad = jnp.pad(b_router.astype(F32), ((0, 0), (0, LANES - N_EXPERTS)), constant_values=-1e30)
    zeros_p = jnp.zeros((bp, S5_NSTATE), F32)

    kp, vp, srp, sip = [], [], [], []
    ksm, vsm, srs, sis = [], [], [], []
    for li in range(DEPTH):
        l = jnp.full((1,), li, jnp.int32)
        ab_re, ab_im, wb, wc = s5_discretise(s5_a_re[li], s5_a_im[li], s5_log_dt[li], s5_b_re[li], s5_b_im[li],
                                             s5_c_re[li], s5_c_im[li])
        prm = dict(w_in=w_in_bf, w_out=w_out_bf, w_glu=w_glu_bf, w_gate_up=wgu_bf, w_down=wd_bf,
                   b_gate_up=b_gate_up, b_down=b_down,
                   ab_re=ab_re, ab_im=ab_im, wb=wb, wc=wc,
                   s5_d=s5_d[li].reshape(1, S5_WIDTH), b_glu=s5_b_glu[li].reshape(1, S5_WIDTH),
                   ln1_g=ln1_g[li].reshape(1, D_MODEL), ln1_b=ln1_b[li].reshape(1, D_MODEL),
                   ln2_g=ln2_g[li].reshape(1, D_MODEL), ln2_b=ln2_b[li].reshape(1, D_MODEL),
                   w_router=w_router_pad[li], b_router=b_router_pad[li].reshape(1, LANES))
        xp, k, v, s_re, s_im = _run_layer(l, xp, mod_all[li, :bp], None, None, zeros_p, zeros_p, prm)
        kp.append(k); vp.append(v); srp.append(s_re); sip.append(s_im)
        xs, k, v, s_re, s_im = _run_layer(l, xs, mod_all[li, bp:], cache_k, cache_v,
                                          state_s5_re[li].reshape(bs, S5_NSTATE),
                                          state_s5_im[li].reshape(bs, S5_NSTATE), prm)
        ksm.append(k); vsm.append(v); srs.append(s_re); sis.append(s_im)
    return (xp, xs, jnp.stack(kp), jnp.stack(vp), jnp.stack(srp), jnp.stack(sip),
            jnp.stack(ksm), jnp.stack(vsm), jnp.stack(srs), jnp.stack(sis))
```

```python
import functools
import math

import jax
import jax.numpy as jnp
from jax import lax
from jax.experimental import pallas as pl
from jax.experimental.pallas import tpu as pltpu

F32 = jnp.float32
BF16 = jnp.bfloat16
HIGHEST = lax.Precision.HIGHEST

D_MODEL = 1024
DEPTH = 4
S5_WIDTH = D_MODEL // 2
S5_GROUP_CH = 16
S5_GROUPS = S5_WIDTH // S5_GROUP_CH
S5_STATE = 64
S5_NSTATE = S5_GROUPS * S5_STATE
SB_WIDTH = D_MODEL - S5_WIDTH
SB_HEAD_DIM = 64
SB_HEADS = SB_WIDTH // SB_HEAD_DIM
N_EXPERTS = 32
TOP_K = 4
D_FF = D_MODEL
SWIGLU_LIMIT = 7.0
SWIGLU_ALPHA = 1.702
LN_EPS = 1e-5
DEEPNORM_ALPHA = (2 * DEPTH) ** 0.25

LANES = 128
SUBLANES = 8
ROW_TILES = D_MODEL // LANES
VMEM_LIMIT = 56 * 1024 * 1024


def _cparams(n_axes):
    return pltpu.CompilerParams(dimension_semantics=("arbitrary",) * n_axes,
                                vmem_limit_bytes=VMEM_LIMIT)


def _layer_norm(x, g, b):
    mu = jnp.mean(x, axis=-1, keepdims=True)
    xc = x - mu
    var = jnp.mean(xc * xc, axis=-1, keepdims=True)
    return xc * lax.rsqrt(var + LN_EPS) * g + b


def _ada_kernel(c_ref, w_ref, b_ref, o_ref):
    c = c_ref[...]
    s = c * jax.nn.sigmoid(c)
    o_ref[0] = jnp.dot(s, w_ref[0], precision=HIGHEST, preferred_element_type=F32) + b_ref[0]


def adaln_all(c_all, w_ada, b_ada):
    bc = c_all.shape[0]
    return pl.pallas_call(
        _ada_kernel,
        grid=(DEPTH, 6),
        in_specs=[pl.BlockSpec((bc, D_MODEL), lambda l, j: (0, 0)),
                  pl.BlockSpec((1, D_MODEL, D_MODEL), lambda l, j: (l, 0, j)),
                  pl.BlockSpec((1, 1, D_MODEL), lambda l, j: (l, 0, j))],
        out_specs=pl.BlockSpec((1, bc, D_MODEL), lambda l, j: (l, 0, j)),
        out_shape=jax.ShapeDtypeStruct((DEPTH, bc, 6 * D_MODEL), F32),
        compiler_params=_cparams(2),
        name="adaln",
    )(c_all, w_ada, b_ada.reshape(DEPTH, 1, 6 * D_MODEL))


def _ln_kernel(x_ref, g_ref, b_ref, o_ref):
    o_ref[...] = _layer_norm(x_ref[...], g_ref[...], b_ref[...])


def ln_rows(x2d, g, b, tm):
    t = x2d.shape[0]
    return pl.pallas_call(
        _ln_kernel,
        grid=(t // tm,),
        in_specs=[pl.BlockSpec((tm, D_MODEL), lambda i: (i, 0)),
                  pl.BlockSpec((1, D_MODEL), lambda i: (0, 0)),
                  pl.BlockSpec((1, D_MODEL), lambda i: (0, 0))],
        out_specs=pl.BlockSpec((tm, D_MODEL), lambda i: (i, 0)),
        out_shape=jax.ShapeDtypeStruct((t, D_MODEL), F32),
        compiler_params=_cparams(1),
        name="ln_in",
    )(x2d, g.reshape(1, D_MODEL), b.reshape(1, D_MODEL))


def _inproj_kernel(l_ref, x_ref, sc_ref, sh_ref, w_ref, k_all_ref, v_all_ref, u_ref, q_ref, k_ref, v_ref):
    del l_ref, k_all_ref, v_all_ref
    h = x_ref[0] * (1.0 + sc_ref[0]) + sh_ref[0]
    proj = jnp.dot(h.astype(BF16), w_ref[0], preferred_element_type=F32)
    u_ref[0] = proj[:, :S5_WIDTH]
    for hh in range(SB_HEADS):
        lo = S5_WIDTH + hh * SB_HEAD_DIM
        q_ref[0, hh] = proj[:, lo:lo + SB_HEAD_DIM]
        k_ref[0, 0, hh] = proj[:, lo + SB_WIDTH:lo + SB_WIDTH + SB_HEAD_DIM]
        v_ref[0, 0, hh] = proj[:, lo + 2 * SB_WIDTH:lo + 2 * SB_WIDTH + SB_HEAD_DIM]


def in_projection(l, x, sc1, sh1, w_in_bf16, k_all, v_all, tm):
    b, L, _ = x.shape
    n_proj = S5_WIDTH + 3 * SB_WIDTH
    mod_spec = pl.BlockSpec((1, 1, D_MODEL), lambda i, j, l_ref: (i, 0, 0))
    any_spec = pl.BlockSpec(memory_space=pl.ANY)
    kv_spec = pl.BlockSpec((1, 1, SB_HEADS, tm, SB_HEAD_DIM), lambda i, j, l_ref: (l_ref[0], i, 0, j, 0))
    kv_shape = jax.ShapeDtypeStruct(k_all.shape, F32)
    return pl.pallas_call(
        _inproj_kernel,
        grid_spec=pltpu.PrefetchScalarGridSpec(
            num_scalar_prefetch=1,
            grid=(b, L // tm),
            in_specs=[pl.BlockSpec((1, tm, D_MODEL), lambda i, j, l_ref: (i, j, 0)),
                      mod_spec, mod_spec,
                      pl.BlockSpec((1, D_MODEL, n_proj), lambda i, j, l_ref: (l_ref[0], 0, 0)),
                      any_spec, any_spec],
            out_specs=[pl.BlockSpec((1, tm, S5_WIDTH), lambda i, j, l_ref: (i, j, 0)),
                       pl.BlockSpec((1, SB_HEADS, tm, SB_HEAD_DIM), lambda i, j, l_ref: (i, 0, j, 0)),
                       kv_spec, kv_spec]),
        out_shape=[jax.ShapeDtypeStruct((b, L, S5_WIDTH), F32),
                   jax.ShapeDtypeStruct((b, SB_HEADS, L, SB_HEAD_DIM), F32), kv_shape, kv_shape],
        input_output_aliases={5: 2, 6: 3},
        compiler_params=_cparams(2),
        name="in_proj",
    )(l, x, sc1, sh1, w_in_bf16, k_all, v_all)


S5_BATCH = SUBLANES
S5_SLABS = S5_NSTATE // LANES
S5_CHUNKS = S5_WIDTH // LANES
S5_CHUNK_STATES = S5_NSTATE // S5_CHUNKS
S5_CHUNK_SLABS = S5_CHUNK_STATES // LANES


def _s5_kernel(l_ref, u_ref, x0r_ref, x0i_ref, wb_ref, ar_ref, ai_ref, wc_ref, d_ref, wg_ref, bg_ref,
               o_ref, sr_ref, si_ref, upad, slab, st_re, st_im, *, lc, pitch):
    del l_ref
    ci = pl.program_id(1)

    @pl.when(ci == 0)
    def _():
        upad[...] = jnp.zeros_like(upad)
        st_re[...] = x0r_ref[...]
        st_im[...] = x0i_ref[...]

    for b in range(S5_BATCH):
        upad[b * pitch:b * pitch + lc, :] = u_ref[b]

    for c in range(S5_CHUNKS):
        r = jnp.dot(upad[:, c * LANES:(c + 1) * LANES].astype(BF16), wb_ref[c],
                    preferred_element_type=F32)
        for j in range(S5_CHUNK_SLABS):
            slab[c * S5_CHUNK_SLABS + j] = r[:, j * LANES:(j + 1) * LANES]
            slab[S5_SLABS + c * S5_CHUNK_SLABS + j] = r[:, S5_CHUNK_STATES + j * LANES:
                                                          S5_CHUNK_STATES + (j + 1) * LANES]

    for c in range(S5_CHUNKS):
        lane0 = c * S5_CHUNK_STATES
        ar = [jnp.broadcast_to(ar_ref[:, lane0 + j * LANES:lane0 + (j + 1) * LANES], (S5_BATCH, LANES))
              for j in range(S5_CHUNK_SLABS)]
        ai = [jnp.broadcast_to(ai_ref[:, lane0 + j * LANES:lane0 + (j + 1) * LANES], (S5_BATCH, LANES))
              for j in range(S5_CHUNK_SLABS)]
        xr0 = tuple(st_re[:, lane0 + j * LANES:lane0 + (j + 1) * LANES] for j in range(S5_CHUNK_SLABS))
        xi0 = tuple(st_im[:, lane0 + j * LANES:lane0 + (j + 1) * LANES] for j in range(S5_CHUNK_SLABS))

        def step(t, carry, c=c, ar=ar, ai=ai):
            xr, xi = carry
            nxr, nxi = [], []
            for j in range(S5_CHUNK_SLABS):
                s_re = c * S5_CHUNK_SLABS + j
                s_im = S5_SLABS + s_re
                rows_t = pl.ds(t, S5_BATCH, stride=pitch)
                nr = ar[j] * xr[j] - ai[j] * xi[j] + slab[s_re, rows_t, :]
                ni = ar[j] * xi[j] + ai[j] * xr[j] + slab[s_im, rows_t, :]
                slab[s_re, rows_t, :] = nr
                slab[s_im, rows_t, :] = ni
                nxr.append(nr)
                nxi.append(ni)
            return tuple(nxr), tuple(nxi)

        xr, xi = lax.fori_loop(0, lc, step, (xr0, xi0))
        for j in range(S5_CHUNK_SLABS):
            st_re[:, lane0 + j * LANES:lane0 + (j + 1) * LANES] = xr[j]
            st_im[:, lane0 + j * LANES:lane0 + (j + 1) * LANES] = xi[j]

    sr_ref[...] = st_re[...]
    si_ref[...] = st_im[...]

    ys = []
    for c in range(S5_CHUNKS):
        xs = jnp.concatenate(
            [slab[c * S5_CHUNK_SLABS + j] for j in range(S5_CHUNK_SLABS)]
            + [slab[S5_SLABS + c * S5_CHUNK_SLABS + j] for j in range(S5_CHUNK_SLABS)], axis=1)
        ys.append(jnp.dot(xs.astype(BF16), wc_ref[c], preferred_element_type=F32))
    y = jnp.concatenate(ys, axis=1) + d_ref[...] * upad[...]
    g = jax.nn.gelu(y)
    gate = jnp.dot(g.astype(BF16), wg_ref[0], preferred_element_type=F32) + bg_ref[...]
    out = g * jax.nn.sigmoid(gate)
    for b in range(S5_BATCH):
        o_ref[b] = out[b * pitch:b * pitch + lc, :]


def s5_mixer(l, u, x0_re, x0_im, wb, ab_re, ab_im, wc, d, w_glu_bf16, b_glu, lc):
    b, L, _ = u.shape
    pitch = lc + SUBLANES
    rows = S5_BATCH * pitch
    kern = functools.partial(_s5_kernel, lc=lc, pitch=pitch)
    const2 = lambda i, j, l_ref: (0, 0)
    const3 = lambda i, j, l_ref: (0, 0, 0)
    state_spec = pl.BlockSpec((S5_BATCH, S5_NSTATE), lambda i, j, l_ref: (i, 0))
    return pl.pallas_call(
        kern,
        grid_spec=pltpu.PrefetchScalarGridSpec(
            num_scalar_prefetch=1,
            grid=(b // S5_BATCH, L // lc),
            in_specs=[pl.BlockSpec((S5_BATCH, lc, S5_WIDTH), lambda i, j, l_ref: (i, j, 0)),
                      state_spec, state_spec,
                      pl.BlockSpec((S5_CHUNKS, LANES, 2 * S5_CHUNK_STATES), const3),
                      pl.BlockSpec((1, S5_NSTATE), const2),
                      pl.BlockSpec((1, S5_NSTATE), const2),
                      pl.BlockSpec((S5_CHUNKS, 2 * S5_CHUNK_STATES, LANES), const3),
                      pl.BlockSpec((1, S5_WIDTH), const2),
                      pl.BlockSpec((1, S5_WIDTH, S5_WIDTH), lambda i, j, l_ref: (l_ref[0], 0, 0)),
                      pl.BlockSpec((1, S5_WIDTH), const2)],
            out_specs=[pl.BlockSpec((S5_BATCH, lc, S5_WIDTH), lambda i, j, l_ref: (i, j, 0)),
                       state_spec, state_spec],
            scratch_shapes=[pltpu.VMEM((rows, S5_WIDTH), F32),
                            pltpu.VMEM((2 * S5_SLABS, rows, LANES), F32),
                            pltpu.VMEM((S5_BATCH, S5_NSTATE), F32),
                            pltpu.VMEM((S5_BATCH, S5_NSTATE), F32)]),
        out_shape=[jax.ShapeDtypeStruct((b, L, S5_WIDTH), F32),
                   jax.ShapeDtypeStruct((b, S5_NSTATE), F32),
                   jax.ShapeDtypeStruct((b, S5_NSTATE), F32)],
        compiler_params=_cparams(2),
        name="s5_mixer",
    )(l, u, x0_re, x0_im, wb, ab_re, ab_im, wc, d, w_glu_bf16, b_glu)


def s5_discretise(a_re, a_im, log_dt, b_re, b_im, c_re, c_im):
    a_re, a_im = a_re.astype(F32), a_im.astype(F32)
    dt = jnp.exp(log_dt.astype(F32))[:, None]
    mag = jnp.exp(a_re * dt)
    ab_re = mag * jnp.cos(a_im * dt)
    ab_im = mag * jnp.sin(a_im * dt)
    den = a_re * a_re + a_im * a_im
    nr, ni = ab_re - 1.0, ab_im
    f_re = (nr * a_re + ni * a_im) / den
    f_im = (ni * a_re - nr * a_im) / den
    b_re, b_im = b_re.astype(F32), b_im.astype(F32)
    bb_re = f_re[..., None] * b_re - f_im[..., None] * b_im
    bb_im = f_re[..., None] * b_im + f_im[..., None] * b_re
    gpc = S5_GROUPS // S5_CHUNKS
    eye = jnp.eye(gpc, dtype=F32)

    def blockdiag_in(bb):
        bb = bb.reshape(S5_CHUNKS, gpc, S5_STATE, S5_GROUP_CH)
        m = jnp.einsum('cgpn,gh->cgnhp', bb, eye)
        return m.reshape(S5_CHUNKS, gpc * S5_GROUP_CH, gpc * S5_STATE)

    def blockdiag_out(cc):
        cc = cc.reshape(S5_CHUNKS, gpc, S5_GROUP_CH, S5_STATE)
        m = jnp.einsum('cgnp,gh->cgphn', cc, eye)
        return m.reshape(S5_CHUNKS, gpc * S5_STATE, gpc * S5_GROUP_CH)

    wb = jnp.concatenate([blockdiag_in(bb_re), blockdiag_in(bb_im)], axis=2).astype(BF16)
    wc = jnp.concatenate([blockdiag_out(c_re.astype(F32)), -blockdiag_out(c_im.astype(F32))],
                         axis=1).astype(BF16)
    return ab_re.reshape(1, S5_NSTATE), ab_im.reshape(1, S5_NSTATE), wb, wc


ATTN_HEADS_PER_STEP = 4
ATTN_KEY_BLOCK = LANES
UNDERFLOW_LOG = -104.0


def _suffix_and_total(width, total_lanes):
    shape = (2 * width, width + total_lanes)
    s_idx = lax.broadcasted_iota(jnp.int32, shape, 0) % width
    j_idx = lax.broadcasted_iota(jnp.int32, shape, 1)
    return jnp.where((s_idx > j_idx) | (j_idx >= width), 1.0, 0.0).astype(BF16)


def _attn_kernel(l_ref, q_ref, ks_ref, vs_ref, kp_ref, vp_ref, o_ref, *, tq, tk, n_past):
    del l_ref
    jd = pl.program_id(2) * (tq // tk) if n_past is None else n_past
    scale = SB_HEAD_DIM ** -0.5
    heads = range(ATTN_HEADS_PER_STEP)
    qs = [(q_ref[0, hh] * scale).astype(BF16) for hh in heads]

    def scores(hh, kb):
        return lax.dot_general(qs[hh], kb.astype(BF16), (((1,), (1,)), ((), ())), preferred_element_type=F32)

    def past_keys(hh, j):
        return kp_ref[0, 0, hh, pl.ds(pl.multiple_of(j * tk, tk), tk), :]

    def past_values(hh, j):
        return vp_ref[0, 0, hh, pl.ds(pl.multiple_of(j * tk, tk), tk), :].astype(BF16)

    def block_terms(z, mask, sums, width):
        sp = jnp.log(1.0 + jnp.exp(-jnp.abs(z)))
        log_not = -(jnp.maximum(z, 0.0) + sp)
        log_beta = jnp.minimum(z, 0.0) - sp
        if mask is not None:
            log_not = jnp.where(mask, log_not, 0.0)
        hi = log_not.astype(BF16)
        lo = (log_not - hi.astype(F32)).astype(BF16)
        cs = jnp.dot(jnp.concatenate([hi, lo], axis=1), sums, preferred_element_type=F32)
        return log_beta, cs[:, :width], cs[:, width:]

    mask = lax.broadcasted_iota(jnp.int32, (tq, tq), 1) < lax.broadcasted_iota(jnp.int32, (tq, tq), 0)
    own_sums = _suffix_and_total(tq, tk)
    carries, accs = [], []
    for hh in heads:
        log_beta, later, total = block_terms(scores(hh, ks_ref[0, 0, hh]), mask, own_sums, tq)
        w = jnp.where(mask, jnp.exp(log_beta + later), 0.0)
        accs.append(jnp.dot(w.astype(BF16), vs_ref[0, 0, hh].astype(BF16), preferred_element_type=F32))
        carries.append(total)
    past_sums = _suffix_and_total(tk, tk)

    def live(carries):
        return jnp.max(functools.reduce(jnp.maximum, carries)) > UNDERFLOW_LOG

    def block_pair(it, carries, accs):
        ja_raw = jd - 1 - 2 * it
        has_a, has_b = ja_raw >= 0, ja_raw >= 1
        ja, jb = jnp.maximum(ja_raw, 0), jnp.maximum(ja_raw - 1, 0)
        new_carries, new_accs = [], []
        for hh in heads:
            lb_a, later_a, total_a = block_terms(scores(hh, past_keys(hh, ja)), None, past_sums, tk)
            lb_b, later_b, total_b = block_terms(scores(hh, past_keys(hh, jb)), None, past_sums, tk)
            carry_b = carries[hh] + jnp.where(has_a, total_a, 0.0)
            w_a = jnp.exp(lb_a + later_a + carries[hh]).astype(BF16)
            w_b = jnp.exp(lb_b + later_b + carry_b).astype(BF16)
            w_a = jnp.where(has_a, w_a, jnp.zeros_like(w_a))
            w_b = jnp.where(has_b, w_b, jnp.zeros_like(w_b))
            new_accs.append(accs[hh]
                            + jnp.dot(w_a, past_values(hh, ja), preferred_element_type=F32)
                            + jnp.dot(w_b, past_values(hh, jb), preferred_element_type=F32))
            new_carries.append(carry_b + jnp.where(has_b, total_b, 0.0))
        return tuple(new_carries), tuple(new_accs)

    def cond(state):
        it, _, _, go = state
        return (it < (jd + 1) // 2) & go

    def body(state):
        it, carries, accs, _ = state
        carries, accs = block_pair(it, carries, accs)
        return it + 1, carries, accs, live(carries)

    carries, accs = block_pair(0, tuple(carries), tuple(accs))
    _, _, accs, _ = lax.while_loop(cond, body, (jnp.int32(1), carries, accs, live(carries)))
    o_ref[0] = jnp.concatenate(accs, axis=1)


def stick_breaking(l, q, k, v, k_past, v_past, tq, tk):
    b, h, L, dh = q.shape
    hp = ATTN_HEADS_PER_STEP
    if k_past is None:
        assert tq % tk == 0
        k_past, v_past, n_past = k, v, None
    else:
        assert L == tq and k_past.shape[3] % tk == 0
        n_past = k_past.shape[3] // tk
    kern = functools.partial(_attn_kernel, tq=tq, tk=tk, n_past=n_past)
    q_spec = pl.BlockSpec((1, hp, tq, dh), lambda i, j, t, l_ref: (i, j, t, 0))
    own_spec = pl.BlockSpec((1, 1, hp, tq, dh), lambda i, j, t, l_ref: (l_ref[0], i, j, t, 0))
    past_spec = pl.BlockSpec((1, 1, hp, k_past.shape[3], dh), lambda i, j, t, l_ref: (l_ref[0], i, j, 0, 0))
    return pl.pallas_call(
        kern,
        grid_spec=pltpu.PrefetchScalarGridSpec(
            num_scalar_prefetch=1,
            grid=(b, h // hp, L // tq),
            in_specs=[q_spec, own_spec, own_spec, past_spec, past_spec],
            out_specs=pl.BlockSpec((1, tq, hp * dh), lambda i, j, t, l_ref: (i, t, j))),
        out_shape=jax.ShapeDtypeStruct((b, L, h * dh), F32),
        compiler_params=_cparams(3),
        name="stick_breaking",
    )(l, q, k, v, k_past, v_past)


def _lane_pack(cols, tm):
    lane = lax.broadcasted_iota(jnp.int32, (tm, LANES), 1)
    out = jnp.zeros((tm, LANES), cols[0].dtype)
    for kk, col in enumerate(cols):
        out = jnp.where(lane == kk, col, out)
    return out


def _outproj_kernel(l_ref, s5_ref, at_ref, x_ref, g1_ref, sc2_ref, sh2_ref, w_ref, lg_ref, lb_ref,
                    wr_ref, br_ref, x1_ref, h2_ref, idx_ref, gate_ref, rank_ref, cnt_ref, counts, *, tm):
    del l_ref
    first = (pl.program_id(0) == 0) & (pl.program_id(1) == 0)

    @pl.when(first)
    def _():
        counts[...] = jnp.zeros_like(counts)

    mix = (jnp.dot(s5_ref[0].astype(BF16), w_ref[0, :S5_WIDTH, :], preferred_element_type=F32)
           + jnp.dot(at_ref[0].astype(BF16), w_ref[0, S5_WIDTH:, :], preferred_element_type=F32))
    x1 = _layer_norm(DEEPNORM_ALPHA * x_ref[0] + (1.0 + g1_ref[0]) * mix, lg_ref[...], lb_ref[...])
    x1_ref[0] = x1
    h2 = x1 * (1.0 + sc2_ref[0]) + sh2_ref[0]
    for s in range(ROW_TILES):
        h2_ref[pl.ds(s, tm, stride=ROW_TILES), :] = h2[:, s * LANES:(s + 1) * LANES]

    logits = jnp.dot(h2, wr_ref[...], precision=HIGHEST, preferred_element_type=F32) + br_ref[...]
    lane = lax.broadcasted_iota(jnp.int32, (tm, LANES), 1)
    work = logits
    vals, idxs, hots = [], [], []
    for _ in range(TOP_K):
        m = jnp.max(work, axis=1, keepdims=True)
        sel = jnp.min(jnp.where(work == m, lane, LANES), axis=1, keepdims=True)
        hot = lane == sel
        work = jnp.where(hot, -jnp.inf, work)
        vals.append(m)
        idxs.append(sel)
        hots.append(hot)
    exps = [jnp.exp(vv - vals[0]) for vv in vals]
    denom = exps[0] + exps[1] + exps[2] + exps[3]
    gates = [e / denom for e in exps]

    hot_any = (hots[0] | hots[1] | hots[2] | hots[3]).astype(F32)
    r_idx = lax.broadcasted_iota(jnp.int32, (tm, tm), 0)
    c_idx = lax.broadcasted_iota(jnp.int32, (tm, tm), 1)
    earlier = jnp.where(c_idx < r_idx, 1.0, 0.0).astype(BF16)
    prefix = jnp.dot(earlier, hot_any.astype(BF16), preferred_element_type=F32) + counts[0:1, :]
    ranks = [jnp.sum(jnp.where(hot, prefix, 0.0), axis=1, keepdims=True).astype(jnp.int32) for hot in hots]
    counts[...] = counts[...] + jnp.sum(hot_any, axis=0, keepdims=True)

    idx_ref[...] = _lane_pack(idxs, tm)
    gate_ref[...] = _lane_pack(gates, tm)
    rank_ref[...] = _lane_pack(ranks, tm)
    cnt_ref[...] = counts[...]


def out_projection(l, s5_out, attn, x, g1, sc2, sh2, w_out_bf16, ln_g, ln_b, w_router_pad, b_router_pad, tm):
    b, L, _ = x.shape
    t = b * L
    nj = L // tm
    kern = functools.partial(_outproj_kernel, tm=tm)
    mod_spec = pl.BlockSpec((1, 1, D_MODEL), lambda i, j, l_ref: (i, 0, 0))
    vec_spec = pl.BlockSpec((1, D_MODEL), lambda i, j, l_ref: (0, 0))
    half_spec = pl.BlockSpec((1, tm, S5_WIDTH), lambda i, j, l_ref: (i, j, 0))
    full_spec = pl.BlockSpec((1, tm, D_MODEL), lambda i, j, l_ref: (i, j, 0))
    tok_spec = pl.BlockSpec((tm, LANES), lambda i, j, l_ref: (i * nj + j, 0))
    return pl.pallas_call(
        kern,
        grid_spec=pltpu.PrefetchScalarGridSpec(
            num_scalar_prefetch=1,
            grid=(b, nj),
            in_specs=[half_spec, half_spec, full_spec, mod_spec, mod_spec, mod_spec,
                      pl.BlockSpec((1, D_MODEL, D_MODEL), lambda i, j, l_ref: (l_ref[0], 0, 0)),
                      vec_spec, vec_spec,
                      pl.BlockSpec((D_MODEL, LANES), lambda i, j, l_ref: (0, 0)),
                      pl.BlockSpec((1, LANES), lambda i, j, l_ref: (0, 0))],
            out_specs=[full_spec,
                       pl.BlockSpec((tm * ROW_TILES, LANES), lambda i, j, l_ref: (i * nj + j, 0)),
                       tok_spec, tok_spec, tok_spec,
                       pl.BlockSpec((SUBLANES, LANES), lambda i, j, l_ref: (0, 0))],
            scratch_shapes=[pltpu.VMEM((SUBLANES, LANES), F32)]),
        out_shape=[jax.ShapeDtypeStruct((b, L, D_MODEL), F32),
                   jax.ShapeDtypeStruct((t * ROW_TILES, LANES), F32),
                   jax.ShapeDtypeStruct((t, LANES), jnp.int32),
                   jax.ShapeDtypeStruct((t, LANES), F32),
                   jax.ShapeDtypeStruct((t, LANES), jnp.int32),
                   jax.ShapeDtypeStruct((SUBLANES, LANES), F32)],
        compiler_params=_cparams(2),
        name="out_proj_router",
    )(l, s5_out, attn, x, g1, sc2, sh2, w_out_bf16, ln_g, ln_b, w_router_pad, b_router_pad)


def _row_copy(src, src_row, dst, dst_row, sem):
    return pltpu.make_async_copy(
        src.at[pl.ds(pl.multiple_of(src_row * ROW_TILES, ROW_TILES), ROW_TILES)],
        dst.at[pl.ds(pl.multiple_of(dst_row * ROW_TILES, ROW_TILES), ROW_TILES)], sem)


def _rows_wait(big_ref, n_rows, sem):
    span = big_ref.at[pl.ds(0, n_rows * ROW_TILES)]
    pltpu.make_async_copy(span, span, sem).wait()


def _dispatch_kernel(lo_ref, hi_ref, dest_ref, h2_ref, xs_ref, zero_tile, sem, fill_sem, *, tm, n_fill):
    n = tm * TOP_K

    @pl.when(pl.program_id(0) == 0)
    def _():
        zero_tile[...] = jnp.zeros_like(zero_tile)

        def fill_range(e, _):
            def fill_row(r, _):
                _row_copy(zero_tile, 0, xs_ref, r, fill_sem).start()
                return 0
            lax.fori_loop(lo_ref[e], hi_ref[e], fill_row, 0)
            return 0

        lax.fori_loop(0, N_EXPERTS + 1, fill_range, 0)
        _rows_wait(xs_ref, n_fill, fill_sem)

    def issue(t, _):
        for kk in range(TOP_K):
            _row_copy(h2_ref, t, xs_ref, dest_ref[t * TOP_K + kk], sem).start(priority=kk % 2)
        return 0

    lax.fori_loop(0, tm, issue, 0, unroll=2)
    _rows_wait(xs_ref, n, sem)


def moe_dispatch(fill_lo, fill_hi, dest_flat, h2_tiles, n_rows, tm):
    n_assign = dest_flat.shape[0]
    kern = functools.partial(_dispatch_kernel, tm=tm, n_fill=n_rows - n_assign)
    return pl.pallas_call(
        kern,
        grid_spec=pltpu.PrefetchScalarGridSpec(
            num_scalar_prefetch=2,
            grid=(n_assign // (tm * TOP_K),),
            in_specs=[pl.BlockSpec((tm * TOP_K,), lambda i, lo, hi: (i,), memory_space=pltpu.SMEM),
                      pl.BlockSpec((tm * ROW_TILES, LANES), lambda i, lo, hi: (i, 0))],
            out_specs=pl.BlockSpec(memory_space=pl.ANY),
            scratch_shapes=[pltpu.VMEM((ROW_TILES, LANES), F32),
                            pltpu.SemaphoreType.DMA(()), pltpu.SemaphoreType.DMA(())]),
        out_shape=jax.ShapeDtypeStruct((n_rows * ROW_TILES, LANES), F32),
        compiler_params=_cparams(1),
        name="moe_dispatch",
    )(fill_lo, fill_hi, dest_flat, h2_tiles)


def _expert_kernel(l_ref, be_ref, nv_ref, xs_ref, wgu_ref, bgu_ref, wd_ref, bd_ref, ys_ref, *, rb):
    del l_ref, be_ref
    nvalid = nv_ref[pl.program_id(0)]

    @pl.when(nvalid > 0)
    def _():
        x = jnp.concatenate([xs_ref[pl.ds(s, rb, stride=ROW_TILES), :] for s in range(ROW_TILES)], axis=1)
        gu = jnp.dot(x.astype(BF16), wgu_ref[0, 0], preferred_element_type=F32) + bgu_ref[0, 0]
        gate = jnp.minimum(gu[:, :D_FF], SWIGLU_LIMIT)
        up = jnp.clip(gu[:, D_FF:], -SWIGLU_LIMIT, SWIGLU_LIMIT)
        act = (up + 1.0) * gate * jax.nn.sigmoid(SWIGLU_ALPHA * gate)
        y = jnp.dot(act.astype(BF16), wd_ref[0, 0], preferred_element_type=F32) + bd_ref[0, 0]
        for s in range(ROW_TILES):
            ys_ref[pl.ds(s, rb, stride=ROW_TILES), :] = y[:, s * LANES:(s + 1) * LANES]

    @pl.when(nvalid <= 0)
    def _():
        ys_ref[...] = jnp.zeros_like(ys_ref)


def moe_experts(l, block_expert, block_valid, xs_tiles, wgu_bf16, b_gate_up, wd_bf16, b_down, rb):
    n_blocks = block_expert.shape[0]
    kern = functools.partial(_expert_kernel, rb=rb)
    row_spec = pl.BlockSpec((rb * ROW_TILES, LANES), lambda i, l_ref, be, nv: (i, 0))
    return pl.pallas_call(
        kern,
        grid_spec=pltpu.PrefetchScalarGridSpec(
            num_scalar_prefetch=3,
            grid=(n_blocks,),
            in_specs=[row_spec,
                      pl.BlockSpec((1, 1, D_MODEL, 2 * D_FF), lambda i, l_ref, be, nv: (l_ref[0], be[i], 0, 0)),
                      pl.BlockSpec((1, 1, 1, 2 * D_FF), lambda i, l_ref, be, nv: (l_ref[0], be[i], 0, 0)),
                      pl.BlockSpec((1, 1, D_FF, D_MODEL), lambda i, l_ref, be, nv: (l_ref[0], be[i], 0, 0)),
                      pl.BlockSpec((1, 1, 1, D_MODEL), lambda i, l_ref, be, nv: (l_ref[0], be[i], 0, 0))],
            out_specs=row_spec),
        out_shape=jax.ShapeDtypeStruct(xs_tiles.shape, F32),
        compiler_params=_cparams(1),
        name="moe_experts",
    )(l, block_expert, block_valid, xs_tiles, wgu_bf16,
      b_gate_up.reshape(DEPTH, N_EXPERTS, 1, 2 * D_FF), wd_bf16, b_down.reshape(DEPTH, N_EXPERTS, 1, D_MODEL))


def _combine_kernel(dest_ref, next_ref, ys_ref, gate_ref, x1_ref, g2_ref, lg_ref, lb_ref, o_ref, buf, sems,
                    *, tm, n_steps):
    n = tm * TOP_K
    step = pl.program_id(0) * pl.num_programs(1) + pl.program_id(1)
    half = step % 2

    def gather(table, into):
        def issue(t, _):
            for kk in range(TOP_K):
                _row_copy(ys_ref, table[t * TOP_K + kk], buf, into * n + kk * tm + t,
                          sems.at[into]).start(priority=kk % 2)
            return 0
        lax.fori_loop(0, tm, issue, 0, unroll=2)

    @pl.when(step == 0)
    def _():
        gather(dest_ref, half)

    @pl.when(step + 1 < n_steps)
    def _():
        gather(next_ref, 1 - half)

    _rows_wait(buf, n, sems.at[half])

    base = half * n * ROW_TILES
    ff = jnp.zeros((tm, D_MODEL), F32)
    for kk in range(TOP_K):
        rows = jnp.concatenate(
            [buf[pl.ds(base + kk * tm * ROW_TILES + s, tm, stride=ROW_TILES), :] for s in range(ROW_TILES)],
            axis=1)
        ff = ff + rows * gate_ref[:, kk:kk + 1]
    o_ref[0] = _layer_norm(DEEPNORM_ALPHA * x1_ref[0] + (1.0 + g2_ref[0]) * ff, lg_ref[...], lb_ref[...])


def moe_combine(dest_flat, ys_tiles, gates, x1, g2, ln_g, ln_b, tm):
    b, L, _ = x1.shape
    nj = L // tm
    n_steps = b * nj
    kern = functools.partial(_combine_kernel, tm=tm, n_steps=n_steps)
    full_spec = pl.BlockSpec((1, tm, D_MODEL), lambda i, j: (i, j, 0))
    vec_spec = pl.BlockSpec((1, D_MODEL), lambda i, j: (0, 0))
    return pl.pallas_call(
        kern,
        grid=(b, nj),
        in_specs=[pl.BlockSpec((tm * TOP_K,), lambda i, j: (i * nj + j,), memory_space=pltpu.SMEM),
                  pl.BlockSpec((tm * TOP_K,), lambda i, j: (jnp.minimum(i * nj + j + 1, n_steps - 1),),
                               memory_space=pltpu.SMEM),
                  pl.BlockSpec(memory_space=pl.ANY),
                  pl.BlockSpec((tm, LANES), lambda i, j: (i * nj + j, 0)),
                  full_spec,
                  pl.BlockSpec((1, 1, D_MODEL), lambda i, j: (i, 0, 0)),
                  vec_spec, vec_spec],
        out_specs=full_spec,
        out_shape=jax.ShapeDtypeStruct((b, L, D_MODEL), F32),
        scratch_shapes=[pltpu.VMEM((2 * tm * TOP_K * ROW_TILES, LANES), F32), pltpu.SemaphoreType.DMA((2,))],
        compiler_params=_cparams(2),
        name="moe_combine",
    )(dest_flat, dest_flat, ys_tiles, gates, x1, g2, ln_g, ln_b)


def _routing_tables(counts, top_idx, rank, rb, n_blocks):
    counts = counts.astype(jnp.int32)
    padded = (counts + rb - 1) // rb * rb
    pend = jnp.cumsum(padded)
    pstart = pend - padded
    dest = (pstart[top_idx] + rank).reshape(-1)
    row0 = jnp.arange(n_blocks, dtype=jnp.int32) * rb
    ends_passed = jnp.sum((pend[None, :] <= row0[:, None]).astype(jnp.int32), axis=1)
    block_expert = jnp.minimum(ends_passed, N_EXPERTS - 1).astype(jnp.int32)
    block_valid = jnp.clip(counts[block_expert] - (row0 - pstart[block_expert]), 0, rb).astype(jnp.int32)
    tail = jnp.full((1,), n_blocks * rb, jnp.int32)
    fill_lo = jnp.concatenate([pstart + counts, pend[-1:]]).astype(jnp.int32)
    fill_hi = jnp.concatenate([pend, tail]).astype(jnp.int32)
    return dest.astype(jnp.int32), block_expert, block_valid, fill_lo, fill_hi


def _tiles(b, L):
    t = b * L
    return dict(
        tm_proj=min(L, 512),
        lc=min(L, 64),
        tq=min(L, 128),
        tm_out=min(L, 256),
        tm_moe=min(L, 256),
        tm_disp=min(t, 512),
        rb=512 if t >= 8192 else 64,
    )


def _run_layer(l, x, mod, k_all, v_all, k_past, v_past, s0_re, s0_im, prm):
    b, L, _ = x.shape
    t = b * L
    tl = _tiles(b, L)
    sh1, sc1, g1, sh2, sc2, g2 = [m[:, None, :] for m in jnp.split(mod, 6, axis=-1)]

    u, q, k, v = in_projection(l, x, sc1, sh1, prm['w_in'], k_all, v_all, tl['tm_proj'])
    s5_out, s_re, s_im = s5_mixer(l, u, s0_re, s0_im, prm['wb'], prm['ab_re'], prm['ab_im'], prm['wc'],
                                  prm['s5_d'], prm['w_glu'], prm['b_glu'], tl['lc'])
    attn = stick_breaking(l, q, k, v, k_past, v_past, tl['tq'], ATTN_KEY_BLOCK)

    x1, h2_tiles, top_idx, gates, rank, counts = out_projection(
        l, s5_out, attn, x, g1, sc2, sh2, prm['w_out'], prm['ln1_g'], prm['ln1_b'],
        prm['w_router'], prm['b_router'], tl['tm_out'])

    rb = tl['rb']
    n_assign = t * TOP_K
    n_blocks = (n_assign + N_EXPERTS * (rb - 1) + rb - 1) // rb
    dest, block_expert, block_valid, fill_lo, fill_hi = _routing_tables(
        counts[0, :N_EXPERTS], top_idx[:, :TOP_K], rank[:, :TOP_K], rb, n_blocks)
    xs_tiles = moe_dispatch(fill_lo, fill_hi, dest, h2_tiles, n_blocks * rb, tl['tm_disp'])
    ys_tiles = moe_experts(l, block_expert, block_valid, xs_tiles, prm['w_gate_up'], prm['b_gate_up'],
                           prm['w_down'], prm['b_down'], rb)
    x2 = moe_combine(dest, ys_tiles, gates, x1, g2, prm['ln2_g'], prm['ln2_b'], tl['tm_moe'])

    hshape = (b, S5_GROUPS, S5_STATE)
    return x2, k, v, s_re.reshape(hshape), s_im.reshape(hshape)


def kernel(x_prompt, x_sample, cache_k, cache_v, state_s5_re, state_s5_im, c_prompt, c_sample, ln_in_g, ln_in_b, w_ada, b_ada, w_in, s5_a_re, s5_a_im, s5_log_dt, s5_b_re, s5_b_im, s5_c_re, s5_c_im, s5_d, s5_w_glu, s5_b_glu, w_out, ln1_g, ln1_b, w_router, b_router, w_gate_up, b_gate_up, w_down, b_down, ln2_g, ln2_b):
    bp, lp, _ = x_prompt.shape
    bs, ls, _ = x_sample.shape

    mod_all = adaln_all(jnp.concatenate([c_prompt, c_sample], axis=0), w_ada, b_ada)
    xp = ln_rows(x_prompt.reshape(bp * lp, D_MODEL), ln_in_g, ln_in_b, min(512, bp * lp)).reshape(bp, lp, D_MODEL)
    xs = ln_rows(x_sample.reshape(bs * ls, D_MODEL), ln_in_g, ln_in_b, min(512, bs * ls)).reshape(bs, ls, D_MODEL)

    w_in_bf = w_in.astype(BF16)
    w_out_bf = w_out.astype(BF16)
    w_glu_bf = s5_w_glu.astype(BF16)
    wgu_bf = w_gate_up.astype(BF16)
    wd_bf = w_down.astype(BF16)
    w_router_pad = jnp.pad(w_router.astype(F32), ((0, 0), (0, 0), (0, LANES - N_EXPERTS)))
    b_router_pad = jnp.pad(b_router.astype(F32), ((0, 0), (0, LANES - N_EXPERTS)), constant_values=-1e30)
    zeros_p = jnp.zeros((bp, S5_NSTATE), F32)

    kp = jnp.zeros((DEPTH, bp, SB_HEADS, lp, SB_HEAD_DIM), F32)
    vp = jnp.zeros((DEPTH, bp, SB_HEADS, lp, SB_HEAD_DIM), F32)
    ksm = jnp.zeros((DEPTH, bs, SB_HEADS, ls, SB_HEAD_DIM), F32)
    vsm = jnp.zeros((DEPTH, bs, SB_HEADS, ls, SB_HEAD_DIM), F32)
    srp, sip, srs, sis = [], [], [], []
    for li in range(DEPTH):
        l = jnp.full((1,), li, jnp.int32)
        ab_re, ab_im, wb, wc = s5_discretise(s5_a_re[li], s5_a_im[li], s5_log_dt[li], s5_b_re[li], s5_b_im[li],
                                             s5_c_re[li], s5_c_im[li])
        prm = dict(w_in=w_in_bf, w_out=w_out_bf, w_glu=w_glu_bf, w_gate_up=wgu_bf, w_down=wd_bf,
                   b_gate_up=b_gate_up, b_down=b_down,
                   ab_re=ab_re, ab_im=ab_im, wb=wb, wc=wc,
                   s5_d=s5_d[li].reshape(1, S5_WIDTH), b_glu=s5_b_glu[li].reshape(1, S5_WIDTH),
                   ln1_g=ln1_g[li].reshape(1, D_MODEL), ln1_b=ln1_b[li].reshape(1, D_MODEL),
                   ln2_g=ln2_g[li].reshape(1, D_MODEL), ln2_b=ln2_b[li].reshape(1, D_MODEL),
                   w_router=w_router_pad[li], b_router=b_router_pad[li].reshape(1, LANES))
        xp, kp, vp, s_re, s_im = _run_layer(l, xp, mod_all[li, :bp], kp, vp, None, None, zeros_p, zeros_p, prm)
        srp.append(s_re); sip.append(s_im)
        xs, ksm, vsm, s_re, s_im = _run_layer(l, xs, mod_all[li, bp:], ksm, vsm, cache_k, cache_v,
                                              state_s5_re[li].reshape(bs, S5_NSTATE),
                                              state_s5_im[li].reshape(bs, S5_NSTATE), prm)
        srs.append(s_re); sis.append(s_im)
    return (xp, xs, kp, vp, jnp.stack(srp), jnp.stack(sip), ksm, vsm, jnp.stack(srs), jnp.stack(sis))
```

```python
import functools
import math

import jax
import jax.numpy as jnp
from jax import lax
from jax.experimental import pallas as pl
from jax.experimental.pallas import tpu as pltpu

F32 = jnp.float32
BF16 = jnp.bfloat16
HIGHEST = lax.Precision.HIGHEST

D_MODEL = 1024
DEPTH = 4
S5_WIDTH = D_MODEL // 2
S5_GROUP_CH = 16
S5_GROUPS = S5_WIDTH // S5_GROUP_CH
S5_STATE = 64
S5_NSTATE = S5_GROUPS * S5_STATE
SB_WIDTH = D_MODEL - S5_WIDTH
SB_HEAD_DIM = 64
SB_HEADS = SB_WIDTH // SB_HEAD_DIM
N_EXPERTS = 32
TOP_K = 4
D_FF = D_MODEL
SWIGLU_LIMIT = 7.0
SWIGLU_ALPHA = 1.702
LN_EPS = 1e-5
DEEPNORM_ALPHA = (2 * DEPTH) ** 0.25

LANES = 128
SUBLANES = 8
ROW_TILES = D_MODEL // LANES
VMEM_LIMIT = 56 * 1024 * 1024


def _cparams(n_axes):
    return pltpu.CompilerParams(dimension_semantics=("arbitrary",) * n_axes,
                                vmem_limit_bytes=VMEM_LIMIT)


def _layer_norm(x, g, b):
    mu = jnp.mean(x, axis=-1, keepdims=True)
    xc = x - mu
    var = jnp.mean(xc * xc, axis=-1, keepdims=True)
    return xc * lax.rsqrt(var + LN_EPS) * g + b


def _ada_kernel(c_ref, w_ref, b_ref, o_ref):
    c = c_ref[...]
    s = c * jax.nn.sigmoid(c)
    o_ref[0] = jnp.dot(s, w_ref[0], precision=HIGHEST, preferred_element_type=F32) + b_ref[0]


def adaln_all(c_all, w_ada, b_ada):
    bc = c_all.shape[0]
    return pl.pallas_call(
        _ada_kernel,
        grid=(DEPTH, 6),
        in_specs=[pl.BlockSpec((bc, D_MODEL), lambda l, j: (0, 0)),
                  pl.BlockSpec((1, D_MODEL, D_MODEL), lambda l, j: (l, 0, j)),
                  pl.BlockSpec((1, 1, D_MODEL), lambda l, j: (l, 0, j))],
        out_specs=pl.BlockSpec((1, bc, D_MODEL), lambda l, j: (l, 0, j)),
        out_shape=jax.ShapeDtypeStruct((DEPTH, bc, 6 * D_MODEL), F32),
        compiler_params=_cparams(2),
        name="adaln",
    )(c_all, w_ada, b_ada.reshape(DEPTH, 1, 6 * D_MODEL))


def _ln_kernel(x_ref, g_ref, b_ref, o_ref):
    o_ref[...] = _layer_norm(x_ref[...], g_ref[...], b_ref[...])


def ln_rows(x2d, g, b, tm):
    t = x2d.shape[0]
    return pl.pallas_call(
        _ln_kernel,
        grid=(t // tm,),
        in_specs=[pl.BlockSpec((tm, D_MODEL), lambda i: (i, 0)),
                  pl.BlockSpec((1, D_MODEL), lambda i: (0, 0)),
                  pl.BlockSpec((1, D_MODEL), lambda i: (0, 0))],
        out_specs=pl.BlockSpec((tm, D_MODEL), lambda i: (i, 0)),
        out_shape=jax.ShapeDtypeStruct((t, D_MODEL), F32),
        compiler_params=_cparams(1),
        name="ln_in",
    )(x2d, g.reshape(1, D_MODEL), b.reshape(1, D_MODEL))


def _inproj_kernel(l_ref, x_ref, sc_ref, sh_ref, w_ref, k_all_ref, v_all_ref, u_ref, q_ref, k_ref, v_ref):
    del l_ref, k_all_ref, v_all_ref
    h = x_ref[0] * (1.0 + sc_ref[0]) + sh_ref[0]
    proj = jnp.dot(h.astype(BF16), w_ref[0], preferred_element_type=F32)
    u_ref[0] = proj[:, :S5_WIDTH]
    for hh in range(SB_HEADS):
        lo = S5_WIDTH + hh * SB_HEAD_DIM
        q_ref[0, hh] = proj[:, lo:lo + SB_HEAD_DIM]
        k_ref[0, 0, hh] = proj[:, lo + SB_WIDTH:lo + SB_WIDTH + SB_HEAD_DIM]
        v_ref[0, 0, hh] = proj[:, lo + 2 * SB_WIDTH:lo + 2 * SB_WIDTH + SB_HEAD_DIM]


def in_projection(l, x, sc1, sh1, w_in_bf16, k_all, v_all, tm):
    b, L, _ = x.shape
    n_proj = S5_WIDTH + 3 * SB_WIDTH
    mod_spec = pl.BlockSpec((1, 1, D_MODEL), lambda i, j, l_ref: (i, 0, 0))
    any_spec = pl.BlockSpec(memory_space=pl.ANY)
    kv_spec = pl.BlockSpec((1, 1, SB_HEADS, tm, SB_HEAD_DIM), lambda i, j, l_ref: (l_ref[0], i, 0, j, 0))
    kv_shape = jax.ShapeDtypeStruct(k_all.shape, F32)
    return pl.pallas_call(
        _inproj_kernel,
        grid_spec=pltpu.PrefetchScalarGridSpec(
            num_scalar_prefetch=1,
            grid=(b, L // tm),
            in_specs=[pl.BlockSpec((1, tm, D_MODEL), lambda i, j, l_ref: (i, j, 0)),
                      mod_spec, mod_spec,
                      pl.BlockSpec((1, D_MODEL, n_proj), lambda i, j, l_ref: (l_ref[0], 0, 0)),
                      any_spec, any_spec],
            out_specs=[pl.BlockSpec((1, tm, S5_WIDTH), lambda i, j, l_ref: (i, j, 0)),
                       pl.BlockSpec((1, SB_HEADS, tm, SB_HEAD_DIM), lambda i, j, l_ref: (i, 0, j, 0)),
                       kv_spec, kv_spec]),
        out_shape=[jax.ShapeDtypeStruct((b, L, S5_WIDTH), F32),
                   jax.ShapeDtypeStruct((b, SB_HEADS, L, SB_HEAD_DIM), F32), kv_shape, kv_shape],
        input_output_aliases={5: 2, 6: 3},
        compiler_params=_cparams(2),
        name="in_proj",
    )(l, x, sc1, sh1, w_in_bf16, k_all, v_all)


S5_BATCH = SUBLANES
S5_SLABS = S5_NSTATE // LANES
S5_CHUNKS = S5_WIDTH // LANES
S5_CHUNK_STATES = S5_NSTATE // S5_CHUNKS
S5_CHUNK_SLABS = S5_CHUNK_STATES // LANES


def _s5_kernel(l_ref, u_ref, x0r_ref, x0i_ref, wb_ref, ar_ref, ai_ref, wc_ref, d_ref, wg_ref, bg_ref,
               o_ref, sr_ref, si_ref, upad, slab, st_re, st_im, *, lc, pitch):
    del l_ref
    ci = pl.program_id(1)

    @pl.when(ci == 0)
    def _():
        upad[...] = jnp.zeros_like(upad)
        st_re[...] = x0r_ref[...]
        st_im[...] = x0i_ref[...]

    for b in range(S5_BATCH):
        upad[b * pitch:b * pitch + lc, :] = u_ref[b]

    for c in range(S5_CHUNKS):
        r = jnp.dot(upad[:, c * LANES:(c + 1) * LANES].astype(BF16), wb_ref[c],
                    preferred_element_type=F32)
        for j in range(S5_CHUNK_SLABS):
            slab[c * S5_CHUNK_SLABS + j] = r[:, j * LANES:(j + 1) * LANES]
            slab[S5_SLABS + c * S5_CHUNK_SLABS + j] = r[:, S5_CHUNK_STATES + j * LANES:
                                                          S5_CHUNK_STATES + (j + 1) * LANES]

    for c in range(S5_CHUNKS):
        lane0 = c * S5_CHUNK_STATES
        ar = [jnp.broadcast_to(ar_ref[:, lane0 + j * LANES:lane0 + (j + 1) * LANES], (S5_BATCH, LANES))
              for j in range(S5_CHUNK_SLABS)]
        ai = [jnp.broadcast_to(ai_ref[:, lane0 + j * LANES:lane0 + (j + 1) * LANES], (S5_BATCH, LANES))
              for j in range(S5_CHUNK_SLABS)]
        xr0 = tuple(st_re[:, lane0 + j * LANES:lane0 + (j + 1) * LANES] for j in range(S5_CHUNK_SLABS))
        xi0 = tuple(st_im[:, lane0 + j * LANES:lane0 + (j + 1) * LANES] for j in range(S5_CHUNK_SLABS))

        def step(t, carry, c=c, ar=ar, ai=ai):
            xr, xi = carry
            nxr, nxi = [], []
            for j in range(S5_CHUNK_SLABS):
                s_re = c * S5_CHUNK_SLABS + j
                s_im = S5_SLABS + s_re
                rows_t = pl.ds(t, S5_BATCH, stride=pitch)
                nr = ar[j] * xr[j] - ai[j] * xi[j] + slab[s_re, rows_t, :]
                ni = ar[j] * xi[j] + ai[j] * xr[j] + slab[s_im, rows_t, :]
                slab[s_re, rows_t, :] = nr
                slab[s_im, rows_t, :] = ni
                nxr.append(nr)
                nxi.append(ni)
            return tuple(nxr), tuple(nxi)

        xr, xi = lax.fori_loop(0, lc, step, (xr0, xi0))
        for j in range(S5_CHUNK_SLABS):
            st_re[:, lane0 + j * LANES:lane0 + (j + 1) * LANES] = xr[j]
            st_im[:, lane0 + j * LANES:lane0 + (j + 1) * LANES] = xi[j]

    sr_ref[...] = st_re[...]
    si_ref[...] = st_im[...]

    ys = []
    for c in range(S5_CHUNKS):
        xs = jnp.concatenate(
            [slab[c * S5_CHUNK_SLABS + j] for j in range(S5_CHUNK_SLABS)]
            + [slab[S5_SLABS + c * S5_CHUNK_SLABS + j] for j in range(S5_CHUNK_SLABS)], axis=1)
        ys.append(jnp.dot(xs.astype(BF16), wc_ref[c], preferred_element_type=F32))
    y = jnp.concatenate(ys, axis=1) + d_ref[...] * upad[...]
    g = jax.nn.gelu(y)
    gate = jnp.dot(g.astype(BF16), wg_ref[0], preferred_element_type=F32) + bg_ref[...]
    out = g * jax.nn.sigmoid(gate)
    for b in range(S5_BATCH):
        o_ref[b] = out[b * pitch:b * pitch + lc, :]


def s5_mixer(l, u, x0_re, x0_im, wb, ab_re, ab_im, wc, d, w_glu_bf16, b_glu, lc):
    b, L, _ = u.shape
    pitch = lc + SUBLANES
    rows = S5_BATCH * pitch
    kern = functools.partial(_s5_kernel, lc=lc, pitch=pitch)
    const2 = lambda i, j, l_ref: (0, 0)
    const3 = lambda i, j, l_ref: (0, 0, 0)
    state_spec = pl.BlockSpec((S5_BATCH, S5_NSTATE), lambda i, j, l_ref: (i, 0))
    return pl.pallas_call(
        kern,
        grid_spec=pltpu.PrefetchScalarGridSpec(
            num_scalar_prefetch=1,
            grid=(b // S5_BATCH, L // lc),
            in_specs=[pl.BlockSpec((S5_BATCH, lc, S5_WIDTH), lambda i, j, l_ref: (i, j, 0)),
                      state_spec, state_spec,
                      pl.BlockSpec((S5_CHUNKS, LANES, 2 * S5_CHUNK_STATES), const3),
                      pl.BlockSpec((1, S5_NSTATE), const2),
                      pl.BlockSpec((1, S5_NSTATE), const2),
                      pl.BlockSpec((S5_CHUNKS, 2 * S5_CHUNK_STATES, LANES), const3),
                      pl.BlockSpec((1, S5_WIDTH), const2),
                      pl.BlockSpec((1, S5_WIDTH, S5_WIDTH), lambda i, j, l_ref: (l_ref[0], 0, 0)),
                      pl.BlockSpec((1, S5_WIDTH), const2)],
            out_specs=[pl.BlockSpec((S5_BATCH, lc, S5_WIDTH), lambda i, j, l_ref: (i, j, 0)),
                       state_spec, state_spec],
            scratch_shapes=[pltpu.VMEM((rows, S5_WIDTH), F32),
                            pltpu.VMEM((2 * S5_SLABS, rows, LANES), F32),
                            pltpu.VMEM((S5_BATCH, S5_NSTATE), F32),
                            pltpu.VMEM((S5_BATCH, S5_NSTATE), F32)]),
        out_shape=[jax.ShapeDtypeStruct((b, L, S5_WIDTH), F32),
                   jax.ShapeDtypeStruct((b, S5_NSTATE), F32),
                   jax.ShapeDtypeStruct((b, S5_NSTATE), F32)],
        compiler_params=_cparams(2),
        name="s5_mixer",
    )(l, u, x0_re, x0_im, wb, ab_re, ab_im, wc, d, w_glu_bf16, b_glu)


def s5_discretise(a_re, a_im, log_dt, b_re, b_im, c_re, c_im):
    a_re, a_im = a_re.astype(F32), a_im.astype(F32)
    dt = jnp.exp(log_dt.astype(F32))[:, None]
    mag = jnp.exp(a_re * dt)
    ab_re = mag * jnp.cos(a_im * dt)
    ab_im = mag * jnp.sin(a_im * dt)
    den = a_re * a_re + a_im * a_im
    nr, ni = ab_re - 1.0, ab_im
    f_re = (nr * a_re + ni * a_im) / den
    f_im = (ni * a_re - nr * a_im) / den
    b_re, b_im = b_re.astype(F32), b_im.astype(F32)
    bb_re = f_re[..., None] * b_re - f_im[..., None] * b_im
    bb_im = f_re[..., None] * b_im + f_im[..., None] * b_re
    gpc = S5_GROUPS // S5_CHUNKS
    eye = jnp.eye(gpc, dtype=F32)

    def blockdiag_in(bb):
        bb = bb.reshape(S5_CHUNKS, gpc, S5_STATE, S5_GROUP_CH)
        m = jnp.einsum('cgpn,gh->cgnhp', bb, eye)
        return m.reshape(S5_CHUNKS, gpc * S5_GROUP_CH, gpc * S5_STATE)

    def blockdiag_out(cc):
        cc = cc.reshape(S5_CHUNKS, gpc, S5_GROUP_CH, S5_STATE)
        m = jnp.einsum('cgnp,gh->cgphn', cc, eye)
        return m.reshape(S5_CHUNKS, gpc * S5_STATE, gpc * S5_GROUP_CH)

    wb = jnp.concatenate([blockdiag_in(bb_re), blockdiag_in(bb_im)], axis=2).astype(BF16)
    wc = jnp.concatenate([blockdiag_out(c_re.astype(F32)), -blockdiag_out(c_im.astype(F32))],
                         axis=1).astype(BF16)
    return ab_re.reshape(1, S5_NSTATE), ab_im.reshape(1, S5_NSTATE), wb, wc


ATTN_HEADS_PER_STEP = 4
ATTN_KEY_BLOCK = LANES
UNDERFLOW_LOG = -104.0


def _suffix_and_total(width, total_lanes):
    shape = (2 * width, width + total_lanes)
    s_idx = lax.broadcasted_iota(jnp.int32, shape, 0) % width
    j_idx = lax.broadcasted_iota(jnp.int32, shape, 1)
    return jnp.where((s_idx > j_idx) | (j_idx >= width), 1.0, 0.0).astype(BF16)


def _attn_kernel(l_ref, q_ref, ks_ref, vs_ref, kp_ref, vp_ref, o_ref, *, tq, tk, n_past):
    del l_ref
    jd = pl.program_id(2) * (tq // tk) if n_past is None else n_past
    scale = SB_HEAD_DIM ** -0.5
    heads = range(ATTN_HEADS_PER_STEP)
    qs = [(q_ref[0, hh] * scale).astype(BF16) for hh in heads]

    def scores(hh, kb):
        return lax.dot_general(qs[hh], kb.astype(BF16), (((1,), (1,)), ((), ())), preferred_element_type=F32)

    def past_keys(hh, j):
        return kp_ref[0, 0, hh, pl.ds(pl.multiple_of(j * tk, tk), tk), :]

    def past_values(hh, j):
        return vp_ref[0, 0, hh, pl.ds(pl.multiple_of(j * tk, tk), tk), :].astype(BF16)

    def block_terms(z, mask, sums, width):
        sp = jnp.log(1.0 + jnp.exp(-jnp.abs(z)))
        log_not = -(jnp.maximum(z, 0.0) + sp)
        log_beta = jnp.minimum(z, 0.0) - sp
        if mask is not None:
            log_not = jnp.where(mask, log_not, 0.0)
        hi = log_not.astype(BF16)
        lo = (log_not - hi.astype(F32)).astype(BF16)
        cs = jnp.dot(jnp.concatenate([hi, lo], axis=1), sums, preferred_element_type=F32)
        return log_beta, cs[:, :width], cs[:, width:]

    mask = lax.broadcasted_iota(jnp.int32, (tq, tq), 1) < lax.broadcasted_iota(jnp.int32, (tq, tq), 0)
    own_sums = _suffix_and_total(tq, tk)
    carries, accs = [], []
    for hh in heads:
        log_beta, later, total = block_terms(scores(hh, ks_ref[0, 0, hh]), mask, own_sums, tq)
        w = jnp.where(mask, jnp.exp(log_beta + later), 0.0)
        accs.append(jnp.dot(w.astype(BF16), vs_ref[0, 0, hh].astype(BF16), preferred_element_type=F32))
        carries.append(total)
    past_sums = _suffix_and_total(tk, tk)

    def live(carries):
        return jnp.max(functools.reduce(jnp.maximum, carries)) > UNDERFLOW_LOG

    def block_pair(it, carries, accs):
        ja_raw = jd - 1 - 2 * it
        has_a, has_b = ja_raw >= 0, ja_raw >= 1
        ja, jb = jnp.maximum(ja_raw, 0), jnp.maximum(ja_raw - 1, 0)
        new_carries, new_accs = [], []
        for hh in heads:
            lb_a, later_a, total_a = block_terms(scores(hh, past_keys(hh, ja)), None, past_sums, tk)
            lb_b, later_b, total_b = block_terms(scores(hh, past_keys(hh, jb)), None, past_sums, tk)
            carry_b = carries[hh] + jnp.where(has_a, total_a, 0.0)
            w_a = jnp.exp(lb_a + later_a + carries[hh]).astype(BF16)
            w_b = jnp.exp(lb_b + later_b + carry_b).astype(BF16)
            w_a = jnp.where(has_a, w_a, jnp.zeros_like(w_a))
            w_b = jnp.where(has_b, w_b, jnp.zeros_like(w_b))
            new_accs.append(accs[hh]
                            + jnp.dot(w_a, past_values(hh, ja), preferred_element_type=F32)
                            + jnp.dot(w_b, past_values(hh, jb), preferred_element_type=F32))
            new_carries.append(carry_b + jnp.where(has_b, total_b, 0.0))
        return tuple(new_carries), tuple(new_accs)

    def cond(state):
        it, _, _, go = state
        return (it < (jd + 1) // 2) & go

    def body(state):
        it, carries, accs, _ = state
        carries, accs = block_pair(it, carries, accs)
        return it + 1, carries, accs, live(carries)

    carries, accs = block_pair(0, tuple(carries), tuple(accs))
    _, _, accs, _ = lax.while_loop(cond, body, (jnp.int32(1), carries, accs, live(carries)))
    o_ref[0] = jnp.concatenate(accs, axis=1)


def stick_breaking(l, q, k, v, k_past, v_past, tq, tk):
    b, h, L, dh = q.shape
    hp = ATTN_HEADS_PER_STEP
    if k_past is None:
        assert tq % tk == 0
        k_past, v_past, n_past = k, v, None
    else:
        assert L == tq and k_past.shape[3] % tk == 0
        n_past = k_past.shape[3] // tk
    kern = functools.partial(_attn_kernel, tq=tq, tk=tk, n_past=n_past)
    q_spec = pl.BlockSpec((1, hp, tq, dh), lambda i, j, t, l_ref: (i, j, t, 0))
    own_spec = pl.BlockSpec((1, 1, hp, tq, dh), lambda i, j, t, l_ref: (l_ref[0], i, j, t, 0))
    past_spec = pl.BlockSpec((1, 1, hp, k_past.shape[3], dh), lambda i, j, t, l_ref: (l_ref[0], i, j, 0, 0))
    return pl.pallas_call(
        kern,
        grid_spec=pltpu.PrefetchScalarGridSpec(
            num_scalar_prefetch=1,
            grid=(b, h // hp, L // tq),
            in_specs=[q_spec, own_spec, own_spec, past_spec, past_spec],
            out_specs=pl.BlockSpec((1, tq, hp * dh), lambda i, j, t, l_ref: (i, t, j))),
        out_shape=jax.ShapeDtypeStruct((b, L, h * dh), F32),
        compiler_params=_cparams(3),
        name="stick_breaking",
    )(l, q, k, v, k_past, v_past)


def _lane_pack(cols, tm):
    lane = lax.broadcasted_iota(jnp.int32, (tm, LANES), 1)
    out = jnp.zeros((tm, LANES), cols[0].dtype)
    for kk, col in enumerate(cols):
        out = jnp.where(lane == kk, col, out)
    return out


def _outproj_kernel(l_ref, s5_ref, at_ref, x_ref, g1_ref, sc2_ref, sh2_ref, w_ref, lg_ref, lb_ref,
                    wr_ref, br_ref, x1_ref, h2_ref, idx_ref, gate_ref, rank_ref, cnt_ref, counts, *, tm):
    del l_ref
    first = (pl.program_id(0) == 0) & (pl.program_id(1) == 0)

    @pl.when(first)
    def _():
        counts[...] = jnp.zeros_like(counts)

    mix = (jnp.dot(s5_ref[0].astype(BF16), w_ref[0, :S5_WIDTH, :], preferred_element_type=F32)
           + jnp.dot(at_ref[0].astype(BF16), w_ref[0, S5_WIDTH:, :], preferred_element_type=F32))
    x1 = _layer_norm(DEEPNORM_ALPHA * x_ref[0] + (1.0 + g1_ref[0]) * mix, lg_ref[...], lb_ref[...])
    x1_ref[0] = x1
    h2 = x1 * (1.0 + sc2_ref[0]) + sh2_ref[0]
    for s in range(ROW_TILES):
        h2_ref[pl.ds(s, tm, stride=ROW_TILES), :] = h2[:, s * LANES:(s + 1) * LANES]

    h_hi = h2.astype(BF16)
    h_lo = (h2 - h_hi.astype(F32)).astype(BF16)
    by_hi = jnp.dot(h_hi, wr_ref[...], preferred_element_type=F32)
    logits = (by_hi[:, :LANES] + by_hi[:, LANES:]
              + jnp.dot(h_lo, wr_ref[:, :LANES], preferred_element_type=F32) + br_ref[...])
    lane = lax.broadcasted_iota(jnp.int32, (tm, LANES), 1)
    work = logits
    vals, idxs, hots = [], [], []
    for _ in range(TOP_K):
        m = jnp.max(work, axis=1, keepdims=True)
        sel = jnp.min(jnp.where(work == m, lane, LANES), axis=1, keepdims=True)
        hot = lane == sel
        work = jnp.where(hot, -jnp.inf, work)
        vals.append(m)
        idxs.append(sel)
        hots.append(hot)
    exps = [jnp.exp(vv - vals[0]) for vv in vals]
    denom = exps[0] + exps[1] + exps[2] + exps[3]
    gates = [e / denom for e in exps]

    hot_any = (hots[0] | hots[1] | hots[2] | hots[3]).astype(F32)
    r_idx = lax.broadcasted_iota(jnp.int32, (tm, tm), 0)
    c_idx = lax.broadcasted_iota(jnp.int32, (tm, tm), 1)
    earlier = jnp.where(c_idx < r_idx, 1.0, 0.0).astype(BF16)
    prefix = jnp.dot(earlier, hot_any.astype(BF16), preferred_element_type=F32) + counts[0:1, :]
    ranks = [jnp.sum(jnp.where(hot, prefix, 0.0), axis=1, keepdims=True).astype(jnp.int32) for hot in hots]
    counts[...] = counts[...] + jnp.sum(hot_any, axis=0, keepdims=True)

    idx_ref[...] = _lane_pack(idxs, tm)
    gate_ref[...] = _lane_pack(gates, tm)
    rank_ref[...] = _lane_pack(ranks, tm)
    cnt_ref[...] = counts[...]


def out_projection(l, s5_out, attn, x, g1, sc2, sh2, w_out_bf16, ln_g, ln_b, w_router_pad, b_router_pad, tm):
    b, L, _ = x.shape
    t = b * L
    nj = L // tm
    kern = functools.partial(_outproj_kernel, tm=tm)
    mod_spec = pl.BlockSpec((1, 1, D_MODEL), lambda i, j, l_ref: (i, 0, 0))
    vec_spec = pl.BlockSpec((1, D_MODEL), lambda i, j, l_ref: (0, 0))
    half_spec = pl.BlockSpec((1, tm, S5_WIDTH), lambda i, j, l_ref: (i, j, 0))
    full_spec = pl.BlockSpec((1, tm, D_MODEL), lambda i, j, l_ref: (i, j, 0))
    tok_spec = pl.BlockSpec((tm, LANES), lambda i, j, l_ref: (i * nj + j, 0))
    return pl.pallas_call(
        kern,
        grid_spec=pltpu.PrefetchScalarGridSpec(
            num_scalar_prefetch=1,
            grid=(b, nj),
            in_specs=[half_spec, half_spec, full_spec, mod_spec, mod_spec, mod_spec,
                      pl.BlockSpec((1, D_MODEL, D_MODEL), lambda i, j, l_ref: (l_ref[0], 0, 0)),
                      vec_spec, vec_spec,
                      pl.BlockSpec((D_MODEL, 2 * LANES), lambda i, j, l_ref: (0, 0)),
                      pl.BlockSpec((1, LANES), lambda i, j, l_ref: (0, 0))],
            out_specs=[full_spec,
                       pl.BlockSpec((tm * ROW_TILES, LANES), lambda i, j, l_ref: (i * nj + j, 0)),
                       tok_spec, tok_spec, tok_spec,
                       pl.BlockSpec((SUBLANES, LANES), lambda i, j, l_ref: (0, 0))],
            scratch_shapes=[pltpu.VMEM((SUBLANES, LANES), F32)]),
        out_shape=[jax.ShapeDtypeStruct((b, L, D_MODEL), F32),
                   jax.ShapeDtypeStruct((t * ROW_TILES, LANES), F32),
                   jax.ShapeDtypeStruct((t, LANES), jnp.int32),
                   jax.ShapeDtypeStruct((t, LANES), F32),
                   jax.ShapeDtypeStruct((t, LANES), jnp.int32),
                   jax.ShapeDtypeStruct((SUBLANES, LANES), F32)],
        compiler_params=_cparams(2),
        name="out_proj_router",
    )(l, s5_out, attn, x, g1, sc2, sh2, w_out_bf16, ln_g, ln_b, w_router_pad, b_router_pad)


def _row_copy(src, src_row, dst, dst_row, sem):
    return pltpu.make_async_copy(
        src.at[pl.ds(pl.multiple_of(src_row * ROW_TILES, ROW_TILES), ROW_TILES)],
        dst.at[pl.ds(pl.multiple_of(dst_row * ROW_TILES, ROW_TILES), ROW_TILES)], sem)


def _rows_wait(big_ref, n_rows, sem):
    span = big_ref.at[pl.ds(0, n_rows * ROW_TILES)]
    pltpu.make_async_copy(span, span, sem).wait()


def _dispatch_kernel(lo_ref, hi_ref, dest_ref, h2_ref, xs_ref, zero_tile, sem, fill_sem, *, tm, n_fill):
    n = tm * TOP_K

    @pl.when(pl.program_id(0) == 0)
    def _():
        zero_tile[...] = jnp.zeros_like(zero_tile)

        def fill_range(e, _):
            def fill_row(r, _):
                _row_copy(zero_tile, 0, xs_ref, r, fill_sem).start()
                return 0
            lax.fori_loop(lo_ref[e], hi_ref[e], fill_row, 0)
            return 0

        lax.fori_loop(0, N_EXPERTS + 1, fill_range, 0)
        _rows_wait(xs_ref, n_fill, fill_sem)

    def issue(t, _):
        for kk in range(TOP_K):
            _row_copy(h2_ref, t, xs_ref, dest_ref[t * TOP_K + kk], sem).start(priority=kk % 2)
        return 0

    lax.fori_loop(0, tm, issue, 0, unroll=2)
    _rows_wait(xs_ref, n, sem)


def moe_dispatch(fill_lo, fill_hi, dest_flat, h2_tiles, n_rows, tm):
    n_assign = dest_flat.shape[0]
    kern = functools.partial(_dispatch_kernel, tm=tm, n_fill=n_rows - n_assign)
    return pl.pallas_call(
        kern,
        grid_spec=pltpu.PrefetchScalarGridSpec(
            num_scalar_prefetch=2,
            grid=(n_assign // (tm * TOP_K),),
            in_specs=[pl.BlockSpec((tm * TOP_K,), lambda i, lo, hi: (i,), memory_space=pltpu.SMEM),
                      pl.BlockSpec((tm * ROW_TILES, LANES), lambda i, lo, hi: (i, 0))],
            out_specs=pl.BlockSpec(memory_space=pl.ANY),
            scratch_shapes=[pltpu.VMEM((ROW_TILES, LANES), F32),
                            pltpu.SemaphoreType.DMA(()), pltpu.SemaphoreType.DMA(())]),
        out_shape=jax.ShapeDtypeStruct((n_rows * ROW_TILES, LANES), F32),
        compiler_params=_cparams(1),
        name="moe_dispatch",
    )(fill_lo, fill_hi, dest_flat, h2_tiles)


def _expert_kernel(l_ref, be_ref, nv_ref, xs_ref, wgu_ref, bgu_ref, wd_ref, bd_ref, ys_ref, *, rb):
    del l_ref, be_ref
    nvalid = nv_ref[pl.program_id(0)]

    @pl.when(nvalid > 0)
    def _():
        x = jnp.concatenate([xs_ref[pl.ds(s, rb, stride=ROW_TILES), :] for s in range(ROW_TILES)], axis=1)
        gu = jnp.dot(x.astype(BF16), wgu_ref[0, 0], preferred_element_type=F32) + bgu_ref[0, 0]
        gate = jnp.minimum(gu[:, :D_FF], SWIGLU_LIMIT)
        up = jnp.clip(gu[:, D_FF:], -SWIGLU_LIMIT, SWIGLU_LIMIT)
        act = (up + 1.0) * gate * jax.nn.sigmoid(SWIGLU_ALPHA * gate)
        y = jnp.dot(act.astype(BF16), wd_ref[0, 0], preferred_element_type=F32) + bd_ref[0, 0]
        for s in range(ROW_TILES):
            ys_ref[pl.ds(s, rb, stride=ROW_TILES), :] = y[:, s * LANES:(s + 1) * LANES]

    @pl.when(nvalid <= 0)
    def _():
        ys_ref[...] = jnp.zeros_like(ys_ref)


def moe_experts(l, block_expert, block_valid, xs_tiles, wgu_bf16, b_gate_up, wd_bf16, b_down, rb):
    n_blocks = block_expert.shape[0]
    kern = functools.partial(_expert_kernel, rb=rb)
    row_spec = pl.BlockSpec((rb * ROW_TILES, LANES), lambda i, l_ref, be, nv: (i, 0))
    return pl.pallas_call(
        kern,
        grid_spec=pltpu.PrefetchScalarGridSpec(
            num_scalar_prefetch=3,
            grid=(n_blocks,),
            in_specs=[row_spec,
                      pl.BlockSpec((1, 1, D_MODEL, 2 * D_FF), lambda i, l_ref, be, nv: (l_ref[0], be[i], 0, 0)),
                      pl.BlockSpec((1, 1, 1, 2 * D_FF), lambda i, l_ref, be, nv: (l_ref[0], be[i], 0, 0)),
                      pl.BlockSpec((1, 1, D_FF, D_MODEL), lambda i, l_ref, be, nv: (l_ref[0], be[i], 0, 0)),
                      pl.BlockSpec((1, 1, 1, D_MODEL), lambda i, l_ref, be, nv: (l_ref[0], be[i], 0, 0))],
            out_specs=row_spec),
        out_shape=jax.ShapeDtypeStruct(xs_tiles.shape, F32),
        compiler_params=_cparams(1),
        name="moe_experts",
    )(l, block_expert, block_valid, xs_tiles, wgu_bf16,
      b_gate_up.reshape(DEPTH, N_EXPERTS, 1, 2 * D_FF), wd_bf16, b_down.reshape(DEPTH, N_EXPERTS, 1, D_MODEL))


COMBINE_CHUNK = 8 * SUBLANES


def _combine_kernel(dest_ref, next_ref, ys_ref, gate_ref, x1_ref, g2_ref, lg_ref, lb_ref, o_ref, buf, sems,
                    *, tm, n_steps):
    n = tm * TOP_K
    step = pl.program_id(0) * pl.num_programs(1) + pl.program_id(1)
    half = step % 2

    def issue(table, into, t):
        for kk in range(TOP_K):
            _row_copy(ys_ref, table[t * TOP_K + kk], buf, into * n + kk * tm + t,
                      sems.at[into]).start(priority=kk % 2)

    @pl.when(step == 0)
    def _():
        def first(t, _):
            issue(dest_ref, half, t)
            return 0
        lax.fori_loop(0, tm, first, 0, unroll=2)

    _rows_wait(buf, n, sems.at[half])

    base = half * n * ROW_TILES
    scale2 = 1.0 + g2_ref[0]

    cs = min(tm, COMBINE_CHUNK)

    def chunk(c, _):
        t0 = pl.multiple_of(c * cs, cs)
        for i in range(cs):
            issue(next_ref, 1 - half, t0 + i)
        gates = gate_ref[pl.ds(t0, cs), :]
        ff = jnp.zeros((cs, D_MODEL), F32)
        for kk in range(TOP_K):
            first_row = base + (kk * tm + t0) * ROW_TILES
            rows = jnp.concatenate(
                [buf[pl.ds(first_row + s, cs, stride=ROW_TILES), :] for s in range(ROW_TILES)], axis=1)
            ff = ff + rows * gates[:, kk:kk + 1]
        y = DEEPNORM_ALPHA * x1_ref[0, pl.ds(t0, cs), :] + scale2 * ff
        o_ref[0, pl.ds(t0, cs), :] = _layer_norm(y, lg_ref[...], lb_ref[...])
        return 0

    assert tm % cs == 0
    lax.fori_loop(0, tm // cs, chunk, 0)

    @pl.when(step == n_steps - 1)
    def _():
        _rows_wait(buf, n, sems.at[1 - half])


def moe_combine(dest_flat, ys_tiles, gates, x1, g2, ln_g, ln_b, tm):
    b, L, _ = x1.shape
    nj = L // tm
    n_steps = b * nj
    kern = functools.partial(_combine_kernel, tm=tm, n_steps=n_steps)
    full_spec = pl.BlockSpec((1, tm, D_MODEL), lambda i, j: (i, j, 0))
    vec_spec = pl.BlockSpec((1, D_MODEL), lambda i, j: (0, 0))
    return pl.pallas_call(
        kern,
        grid=(b, nj),
        in_specs=[pl.BlockSpec((tm * TOP_K,), lambda i, j: (i * nj + j,), memory_space=pltpu.SMEM),
                  pl.BlockSpec((tm * TOP_K,), lambda i, j: (jnp.minimum(i * nj + j + 1, n_steps - 1),),
                               memory_space=pltpu.SMEM),
                  pl.BlockSpec(memory_space=pl.ANY),
                  pl.BlockSpec((tm, LANES), lambda i, j: (i * nj + j, 0)),
                  full_spec,
                  pl.BlockSpec((1, 1, D_MODEL), lambda i, j: (i, 0, 0)),
                  vec_spec, vec_spec],
        out_specs=full_spec,
        out_shape=jax.ShapeDtypeStruct((b, L, D_MODEL), F32),
        scratch_shapes=[pltpu.VMEM((2 * tm * TOP_K * ROW_TILES, LANES), F32), pltpu.SemaphoreType.DMA((2,))],
        compiler_params=_cparams(2),
        name="moe_combine",
    )(dest_flat, dest_flat, ys_tiles, gates, x1, g2, ln_g, ln_b)


def _routing_tables(counts, top_idx, rank, rb, n_blocks):
    counts = counts.astype(jnp.int32)
    padded = (counts + rb - 1) // rb * rb
    pend = jnp.cumsum(padded)
    pstart = pend - padded
    dest = (pstart[top_idx] + rank).reshape(-1)
    row0 = jnp.arange(n_blocks, dtype=jnp.int32) * rb
    ends_passed = jnp.sum((pend[None, :] <= row0[:, None]).astype(jnp.int32), axis=1)
    block_expert = jnp.minimum(ends_passed, N_EXPERTS - 1).astype(jnp.int32)
    block_valid = jnp.clip(counts[block_expert] - (row0 - pstart[block_expert]), 0, rb).astype(jnp.int32)
    tail = jnp.full((1,), n_blocks * rb, jnp.int32)
    fill_lo = jnp.concatenate([pstart + counts, pend[-1:]]).astype(jnp.int32)
    fill_hi = jnp.concatenate([pend, tail]).astype(jnp.int32)
    return dest.astype(jnp.int32), block_expert, block_valid, fill_lo, fill_hi


def _tiles(b, L):
    t = b * L
    return dict(
        tm_proj=min(L, 512),
        lc=min(L, 64),
        tq=min(L, 128),
        tm_out=min(L, 512),
        tm_moe=min(L, 256),
        tm_disp=min(t, 512),
        rb=512 if t >= 8192 else 64,
    )


def _run_layer(l, x, mod, k_all, v_all, k_past, v_past, s0_re, s0_im, prm):
    b, L, _ = x.shape
    t = b * L
    tl = _tiles(b, L)
    sh1, sc1, g1, sh2, sc2, g2 = [m[:, None, :] for m in jnp.split(mod, 6, axis=-1)]

    u, q, k, v = in_projection(l, x, sc1, sh1, prm['w_in'], k_all, v_all, tl['tm_proj'])
    s5_out, s_re, s_im = s5_mixer(l, u, s0_re, s0_im, prm['wb'], prm['ab_re'], prm['ab_im'], prm['wc'],
                                  prm['s5_d'], prm['w_glu'], prm['b_glu'], tl['lc'])
    attn = stick_breaking(l, q, k, v, k_past, v_past, tl['tq'], ATTN_KEY_BLOCK)

    x1, h2_tiles, top_idx, gates, rank, counts = out_projection(
        l, s5_out, attn, x, g1, sc2, sh2, prm['w_out'], prm['ln1_g'], prm['ln1_b'],
        prm['w_router'], prm['b_router'], tl['tm_out'])

    rb = tl['rb']
    n_assign = t * TOP_K
    n_blocks = (n_assign + N_EXPERTS * (rb - 1) + rb - 1) // rb
    dest, block_expert, block_valid, fill_lo, fill_hi = _routing_tables(
        counts[0, :N_EXPERTS], top_idx[:, :TOP_K], rank[:, :TOP_K], rb, n_blocks)
    xs_tiles = moe_dispatch(fill_lo, fill_hi, dest, h2_tiles, n_blocks * rb, tl['tm_disp'])
    ys_tiles = moe_experts(l, block_expert, block_valid, xs_tiles, prm['w_gate_up'], prm['b_gate_up'],
                           prm['w_down'], prm['b_down'], rb)
    x2 = moe_combine(dest, ys_tiles, gates, x1, g2, prm['ln2_g'], prm['ln2_b'], tl['tm_moe'])

    hshape = (b, S5_GROUPS, S5_STATE)
    return x2, k, v, s_re.reshape(hshape), s_im.reshape(hshape)


def kernel(x_prompt, x_sample, cache_k, cache_v, state_s5_re, state_s5_im, c_prompt, c_sample, ln_in_g, ln_in_b, w_ada, b_ada, w_in, s5_a_re, s5_a_im, s5_log_dt, s5_b_re, s5_b_im, s5_c_re, s5_c_im, s5_d, s5_w_glu, s5_b_glu, w_out, ln1_g, ln1_b, w_router, b_router, w_gate_up, b_gate_up, w_down, b_down, ln2_g, ln2_b):
    bp, lp, _ = x_prompt.shape
    bs, ls, _ = x_sample.shape

    mod_all = adaln_all(jnp.concatenate([c_prompt, c_sample], axis=0), w_ada, b_ada)
    xp = ln_rows(x_prompt.reshape(bp * lp, D_MODEL), ln_in_g, ln_in_b, min(512, bp * lp)).reshape(bp, lp, D_MODEL)
    xs = ln_rows(x_sample.reshape(bs * ls, D_MODEL), ln_in_g, ln_in_b, min(512, bs * ls)).reshape(bs, ls, D_MODEL)

    w_in_bf = w_in.astype(BF16)
    w_out_bf = w_out.astype(BF16)
    w_glu_bf = s5_w_glu.astype(BF16)
    wgu_bf = w_gate_up.astype(BF16)
    wd_bf = w_down.astype(BF16)
    w_router_pad = jnp.pad(w_router.astype(F32), ((0, 0), (0, 0), (0, LANES - N_EXPERTS)))
    w_router_hi = w_router_pad.astype(BF16)
    w_router_lo = (w_router_pad - w_router_hi.astype(F32)).astype(BF16)
    w_router_pad = jnp.concatenate([w_router_hi, w_router_lo], axis=2)
    b_router_pad = jnp.pad(b_router.astype(F32), ((0, 0), (0, LANES - N_EXPERTS)), constant_values=-1e30)
    zeros_p = jnp.zeros((bp, S5_NSTATE), F32)

    kp = jnp.zeros((DEPTH, bp, SB_HEADS, lp, SB_HEAD_DIM), F32)
    vp = jnp.zeros((DEPTH, bp, SB_HEADS, lp, SB_HEAD_DIM), F32)
    ksm = jnp.zeros((DEPTH, bs, SB_HEADS, ls, SB_HEAD_DIM), F32)
    vsm = jnp.zeros((DEPTH, bs, SB_HEADS, ls, SB_HEAD_DIM), F32)
    srp, sip, srs, sis = [], [], [], []
    for li in range(DEPTH):
        l = jnp.full((1,), li, jnp.int32)
        ab_re, ab_im, wb, wc = s5_discretise(s5_a_re[li], s5_a_im[li], s5_log_dt[li], s5_b_re[li], s5_b_im[li],
                                             s5_c_re[li], s5_c_im[li])
        prm = dict(w_in=w_in_bf, w_out=w_out_bf, w_glu=w_glu_bf, w_gate_up=wgu_bf, w_down=wd_bf,
                   b_gate_up=b_gate_up, b_down=b_down,
                   ab_re=ab_re, ab_im=ab_im, wb=wb, wc=wc,
                   s5_d=s5_d[li].reshape(1, S5_WIDTH), b_glu=s5_b_glu[li].reshape(1, S5_WIDTH),
                   ln1_g=ln1_g[li].reshape(1, D_MODEL), ln1_b=ln1_b[li].reshape(1, D_MODEL),
                   ln2_g=ln2_g[li].reshape(1, D_MODEL), ln2_b=ln2_b[li].reshape(1, D_MODEL),
                   w_router=w_router_pad[li], b_router=b_router_pad[li].reshape(1, LANES))
        xp, kp, vp, s_re, s_im = _run_layer(l, xp, mod_all[li, :bp], kp, vp, None, None, zeros_p, zeros_p, prm)
        srp.append(s_re); sip.append(s_im)
        xs, ksm, vsm, s_re, s_im = _run_layer(l, xs, mod_all[li, bp:], ksm, vsm, cache_k, cache_v,
                                              state_s5_re[li].reshape(bs, S5_NSTATE),
                                              state_s5_im[li].reshape(bs, S5_NSTATE), prm)
        srs.append(s_re); sis.append(s_im)
    return (xp, xs, kp, vp, jnp.stack(srp), jnp.stack(sip), ksm, vsm, jnp.stack(srs), jnp.stack(sis))
```

```python
import functools
import math

import jax
import jax.numpy as jnp
from jax import lax
from jax.experimental import pallas as pl
from jax.experimental.pallas import tpu as pltpu

F32 = jnp.float32
BF16 = jnp.bfloat16
HIGHEST = lax.Precision.HIGHEST

D_MODEL = 1024
DEPTH = 4
S5_WIDTH = D_MODEL // 2
S5_GROUP_CH = 16
S5_GROUPS = S5_WIDTH // S5_GROUP_CH
S5_STATE = 64
S5_NSTATE = S5_GROUPS * S5_STATE
SB_WIDTH = D_MODEL - S5_WIDTH
SB_HEAD_DIM = 64
SB_HEADS = SB_WIDTH // SB_HEAD_DIM
N_EXPERTS = 32
TOP_K = 4
D_FF = D_MODEL
SWIGLU_LIMIT = 7.0
SWIGLU_ALPHA = 1.702
LN_EPS = 1e-5
DEEPNORM_ALPHA = (2 * DEPTH) ** 0.25

LANES = 128
SUBLANES = 8
ROW_TILES = D_MODEL // LANES
VMEM_LIMIT = 56 * 1024 * 1024


def _cparams(n_axes):
    return pltpu.CompilerParams(dimension_semantics=("arbitrary",) * n_axes,
                                vmem_limit_bytes=VMEM_LIMIT)


def _layer_norm(x, g, b):
    mu = jnp.mean(x, axis=-1, keepdims=True)
    xc = x - mu
    var = jnp.mean(xc * xc, axis=-1, keepdims=True)
    return xc * lax.rsqrt(var + LN_EPS) * g + b


def _ada_kernel(c_ref, w_ref, b_ref, o_ref):
    c = c_ref[...]
    s = c * jax.nn.sigmoid(c)
    o_ref[0] = jnp.dot(s, w_ref[0], precision=HIGHEST, preferred_element_type=F32) + b_ref[0]


def adaln_all(c_all, w_ada, b_ada):
    bc = c_all.shape[0]
    return pl.pallas_call(
        _ada_kernel,
        grid=(DEPTH, 6),
        in_specs=[pl.BlockSpec((bc, D_MODEL), lambda l, j: (0, 0)),
                  pl.BlockSpec((1, D_MODEL, D_MODEL), lambda l, j: (l, 0, j)),
                  pl.BlockSpec((1, 1, D_MODEL), lambda l, j: (l, 0, j))],
        out_specs=pl.BlockSpec((1, bc, D_MODEL), lambda l, j: (l, 0, j)),
        out_shape=jax.ShapeDtypeStruct((DEPTH, bc, 6 * D_MODEL), F32),
        compiler_params=_cparams(2),
        name="adaln",
    )(c_all, w_ada, b_ada.reshape(DEPTH, 1, 6 * D_MODEL))


def _ln_kernel(x_ref, g_ref, b_ref, o_ref):
    o_ref[...] = _layer_norm(x_ref[...], g_ref[...], b_ref[...])


def ln_rows(x2d, g, b, tm):
    t = x2d.shape[0]
    return pl.pallas_call(
        _ln_kernel,
        grid=(t // tm,),
        in_specs=[pl.BlockSpec((tm, D_MODEL), lambda i: (i, 0)),
                  pl.BlockSpec((1, D_MODEL), lambda i: (0, 0)),
                  pl.BlockSpec((1, D_MODEL), lambda i: (0, 0))],
        out_specs=pl.BlockSpec((tm, D_MODEL), lambda i: (i, 0)),
        out_shape=jax.ShapeDtypeStruct((t, D_MODEL), F32),
        compiler_params=_cparams(1),
        name="ln_in",
    )(x2d, g.reshape(1, D_MODEL), b.reshape(1, D_MODEL))


def _inproj_kernel(l_ref, x_ref, sc_ref, sh_ref, w_ref, k_all_ref, v_all_ref, u_ref, q_ref, k_ref, v_ref):
    del l_ref, k_all_ref, v_all_ref
    h = x_ref[0] * (1.0 + sc_ref[0]) + sh_ref[0]
    proj = jnp.dot(h.astype(BF16), w_ref[0], preferred_element_type=F32)
    u_ref[0] = proj[:, :S5_WIDTH]
    for hh in range(SB_HEADS):
        lo = S5_WIDTH + hh * SB_HEAD_DIM
        q_ref[0, hh] = proj[:, lo:lo + SB_HEAD_DIM]
        k_ref[0, 0, hh] = proj[:, lo + SB_WIDTH:lo + SB_WIDTH + SB_HEAD_DIM]
        v_ref[0, 0, hh] = proj[:, lo + 2 * SB_WIDTH:lo + 2 * SB_WIDTH + SB_HEAD_DIM]


def in_projection(l, x, sc1, sh1, w_in_bf16, k_all, v_all, tm):
    b, L, _ = x.shape
    n_proj = S5_WIDTH + 3 * SB_WIDTH
    mod_spec = pl.BlockSpec((1, 1, D_MODEL), lambda i, j, l_ref: (i, 0, 0))
    any_spec = pl.BlockSpec(memory_space=pl.ANY)
    kv_spec = pl.BlockSpec((1, 1, SB_HEADS, tm, SB_HEAD_DIM), lambda i, j, l_ref: (l_ref[0], i, 0, j, 0))
    kv_shape = jax.ShapeDtypeStruct(k_all.shape, F32)
    return pl.pallas_call(
        _inproj_kernel,
        grid_spec=pltpu.PrefetchScalarGridSpec(
            num_scalar_prefetch=1,
            grid=(b, L // tm),
            in_specs=[pl.BlockSpec((1, tm, D_MODEL), lambda i, j, l_ref: (i, j, 0)),
                      mod_spec, mod_spec,
                      pl.BlockSpec((1, D_MODEL, n_proj), lambda i, j, l_ref: (l_ref[0], 0, 0)),
                      any_spec, any_spec],
            out_specs=[pl.BlockSpec((1, tm, S5_WIDTH), lambda i, j, l_ref: (i, j, 0)),
                       pl.BlockSpec((1, SB_HEADS, tm, SB_HEAD_DIM), lambda i, j, l_ref: (i, 0, j, 0)),
                       kv_spec, kv_spec]),
        out_shape=[jax.ShapeDtypeStruct((b, L, S5_WIDTH), F32),
                   jax.ShapeDtypeStruct((b, SB_HEADS, L, SB_HEAD_DIM), F32), kv_shape, kv_shape],
        input_output_aliases={5: 2, 6: 3},
        compiler_params=_cparams(2),
        name="in_proj",
    )(l, x, sc1, sh1, w_in_bf16, k_all, v_all)


S5_BATCH = SUBLANES
S5_SLABS = S5_NSTATE // LANES
S5_CHUNKS = S5_WIDTH // LANES
S5_CHUNK_STATES = S5_NSTATE // S5_CHUNKS
S5_CHUNK_SLABS = S5_CHUNK_STATES // LANES


def _s5_kernel(l_ref, u_ref, x0r_ref, x0i_ref, wb_ref, ar_ref, ai_ref, wc_ref, d_ref, wg_ref, bg_ref,
               o_ref, sr_ref, si_ref, upad, slab, st_re, st_im, *, lc, pitch):
    del l_ref
    ci = pl.program_id(1)

    @pl.when(ci == 0)
    def _():
        upad[...] = jnp.zeros_like(upad)
        st_re[...] = x0r_ref[...]
        st_im[...] = x0i_ref[...]

    for b in range(S5_BATCH):
        upad[b * pitch:b * pitch + lc, :] = u_ref[b]

    for c in range(S5_CHUNKS):
        r = jnp.dot(upad[:, c * LANES:(c + 1) * LANES].astype(BF16), wb_ref[c],
                    preferred_element_type=F32)
        for j in range(S5_CHUNK_SLABS):
            slab[c * S5_CHUNK_SLABS + j] = r[:, j * LANES:(j + 1) * LANES]
            slab[S5_SLABS + c * S5_CHUNK_SLABS + j] = r[:, S5_CHUNK_STATES + j * LANES:
                                                          S5_CHUNK_STATES + (j + 1) * LANES]

    for c in range(S5_CHUNKS):
        lane0 = c * S5_CHUNK_STATES
        ar = [jnp.broadcast_to(ar_ref[:, lane0 + j * LANES:lane0 + (j + 1) * LANES], (S5_BATCH, LANES))
              for j in range(S5_CHUNK_SLABS)]
        ai = [jnp.broadcast_to(ai_ref[:, lane0 + j * LANES:lane0 + (j + 1) * LANES], (S5_BATCH, LANES))
              for j in range(S5_CHUNK_SLABS)]
        xr0 = tuple(st_re[:, lane0 + j * LANES:lane0 + (j + 1) * LANES] for j in range(S5_CHUNK_SLABS))
        xi0 = tuple(st_im[:, lane0 + j * LANES:lane0 + (j + 1) * LANES] for j in range(S5_CHUNK_SLABS))

        def step(t, carry, c=c, ar=ar, ai=ai):
            xr, xi = carry
            nxr, nxi = [], []
            for j in range(S5_CHUNK_SLABS):
                s_re = c * S5_CHUNK_SLABS + j
                s_im = S5_SLABS + s_re
                rows_t = pl.ds(t, S5_BATCH, stride=pitch)
                nr = ar[j] * xr[j] - ai[j] * xi[j] + slab[s_re, rows_t, :]
                ni = ar[j] * xi[j] + ai[j] * xr[j] + slab[s_im, rows_t, :]
                slab[s_re, rows_t, :] = nr
                slab[s_im, rows_t, :] = ni
                nxr.append(nr)
                nxi.append(ni)
            return tuple(nxr), tuple(nxi)

        xr, xi = lax.fori_loop(0, lc, step, (xr0, xi0), unroll=2)
        for j in range(S5_CHUNK_SLABS):
            st_re[:, lane0 + j * LANES:lane0 + (j + 1) * LANES] = xr[j]
            st_im[:, lane0 + j * LANES:lane0 + (j + 1) * LANES] = xi[j]

    sr_ref[...] = st_re[...]
    si_ref[...] = st_im[...]

    ys = []
    for c in range(S5_CHUNKS):
        xs = jnp.concatenate(
            [slab[c * S5_CHUNK_SLABS + j] for j in range(S5_CHUNK_SLABS)]
            + [slab[S5_SLABS + c * S5_CHUNK_SLABS + j] for j in range(S5_CHUNK_SLABS)], axis=1)
        ys.append(jnp.dot(xs.astype(BF16), wc_ref[c], preferred_element_type=F32))
    y = jnp.concatenate(ys, axis=1) + d_ref[...] * upad[...]
    g = jax.nn.gelu(y)
    gate = jnp.dot(g.astype(BF16), wg_ref[0], preferred_element_type=F32) + bg_ref[...]
    out = g * jax.nn.sigmoid(gate)
    for b in range(S5_BATCH):
        o_ref[b] = out[b * pitch:b * pitch + lc, :]


def s5_mixer(l, u, x0_re, x0_im, wb, ab_re, ab_im, wc, d, w_glu_bf16, b_glu, lc):
    b, L, _ = u.shape
    pitch = lc + SUBLANES
    rows = S5_BATCH * pitch
    kern = functools.partial(_s5_kernel, lc=lc, pitch=pitch)
    const2 = lambda i, j, l_ref: (0, 0)
    const3 = lambda i, j, l_ref: (0, 0, 0)
    state_spec = pl.BlockSpec((S5_BATCH, S5_NSTATE), lambda i, j, l_ref: (i, 0))
    return pl.pallas_call(
        kern,
        grid_spec=pltpu.PrefetchScalarGridSpec(
            num_scalar_prefetch=1,
            grid=(b // S5_BATCH, L // lc),
            in_specs=[pl.BlockSpec((S5_BATCH, lc, S5_WIDTH), lambda i, j, l_ref: (i, j, 0)),
                      state_spec, state_spec,
                      pl.BlockSpec((S5_CHUNKS, LANES, 2 * S5_CHUNK_STATES), const3),
                      pl.BlockSpec((1, S5_NSTATE), const2),
                      pl.BlockSpec((1, S5_NSTATE), const2),
                      pl.BlockSpec((S5_CHUNKS, 2 * S5_CHUNK_STATES, LANES), const3),
                      pl.BlockSpec((1, S5_WIDTH), const2),
                      pl.BlockSpec((1, S5_WIDTH, S5_WIDTH), lambda i, j, l_ref: (l_ref[0], 0, 0)),
                      pl.BlockSpec((1, S5_WIDTH), const2)],
            out_specs=[pl.BlockSpec((S5_BATCH, lc, S5_WIDTH), lambda i, j, l_ref: (i, j, 0)),
                       state_spec, state_spec],
            scratch_shapes=[pltpu.VMEM((rows, S5_WIDTH), F32),
                            pltpu.VMEM((2 * S5_SLABS, rows, LANES), F32),
                            pltpu.VMEM((S5_BATCH, S5_NSTATE), F32),
                            pltpu.VMEM((S5_BATCH, S5_NSTATE), F32)]),
        out_shape=[jax.ShapeDtypeStruct((b, L, S5_WIDTH), F32),
                   jax.ShapeDtypeStruct((b, S5_NSTATE), F32),
                   jax.ShapeDtypeStruct((b, S5_NSTATE), F32)],
        compiler_params=_cparams(2),
        name="s5_mixer",
    )(l, u, x0_re, x0_im, wb, ab_re, ab_im, wc, d, w_glu_bf16, b_glu)


def s5_discretise(a_re, a_im, log_dt, b_re, b_im, c_re, c_im):
    a_re, a_im = a_re.astype(F32), a_im.astype(F32)
    dt = jnp.exp(log_dt.astype(F32))[:, None]
    mag = jnp.exp(a_re * dt)
    ab_re = mag * jnp.cos(a_im * dt)
    ab_im = mag * jnp.sin(a_im * dt)
    den = a_re * a_re + a_im * a_im
    nr, ni = ab_re - 1.0, ab_im
    f_re = (nr * a_re + ni * a_im) / den
    f_im = (ni * a_re - nr * a_im) / den
    b_re, b_im = b_re.astype(F32), b_im.astype(F32)
    bb_re = f_re[..., None] * b_re - f_im[..., None] * b_im
    bb_im = f_re[..., None] * b_im + f_im[..., None] * b_re
    gpc = S5_GROUPS // S5_CHUNKS
    eye = jnp.eye(gpc, dtype=F32)

    def blockdiag_in(bb):
        bb = bb.reshape(S5_CHUNKS, gpc, S5_STATE, S5_GROUP_CH)
        m = jnp.einsum('cgpn,gh->cgnhp', bb, eye)
        return m.reshape(S5_CHUNKS, gpc * S5_GROUP_CH, gpc * S5_STATE)

    def blockdiag_out(cc):
        cc = cc.reshape(S5_CHUNKS, gpc, S5_GROUP_CH, S5_STATE)
        m = jnp.einsum('cgnp,gh->cgphn', cc, eye)
        return m.reshape(S5_CHUNKS, gpc * S5_STATE, gpc * S5_GROUP_CH)

    wb = jnp.concatenate([blockdiag_in(bb_re), blockdiag_in(bb_im)], axis=2).astype(BF16)
    wc = jnp.concatenate([blockdiag_out(c_re.astype(F32)), -blockdiag_out(c_im.astype(F32))],
                         axis=1).astype(BF16)
    return ab_re.reshape(1, S5_NSTATE), ab_im.reshape(1, S5_NSTATE), wb, wc


ATTN_HEADS_PER_STEP = 4
ATTN_KEY_BLOCK = LANES
UNDERFLOW_LOG = -104.0


def _suffix_and_total(width, total_lanes):
    shape = (2 * width, width + total_lanes)
    s_idx = lax.broadcasted_iota(jnp.int32, shape, 0) % width
    j_idx = lax.broadcasted_iota(jnp.int32, shape, 1)
    return jnp.where((s_idx > j_idx) | (j_idx >= width), 1.0, 0.0).astype(BF16)


def _attn_kernel(l_ref, q_ref, ks_ref, vs_ref, kp_ref, vp_ref, o_ref, *, tq, tk, n_past):
    del l_ref
    jd = pl.program_id(2) * (tq // tk) if n_past is None else n_past
    scale = SB_HEAD_DIM ** -0.5
    heads = range(ATTN_HEADS_PER_STEP)
    qs = [(q_ref[0, hh] * scale).astype(BF16) for hh in heads]

    def scores(hh, kb):
        return lax.dot_general(qs[hh], kb.astype(BF16), (((1,), (1,)), ((), ())), preferred_element_type=F32)

    def past_keys(hh, j):
        return kp_ref[0, 0, hh, pl.ds(pl.multiple_of(j * tk, tk), tk), :]

    def past_values(hh, j):
        return vp_ref[0, 0, hh, pl.ds(pl.multiple_of(j * tk, tk), tk), :].astype(BF16)

    def block_terms(z, mask, sums, width):
        sp = jnp.log(1.0 + jnp.exp(-jnp.abs(z)))
        log_not = -(jnp.maximum(z, 0.0) + sp)
        log_beta = jnp.minimum(z, 0.0) - sp
        if mask is not None:
            log_not = jnp.where(mask, log_not, 0.0)
        hi = log_not.astype(BF16)
        lo = (log_not - hi.astype(F32)).astype(BF16)
        cs = jnp.dot(jnp.concatenate([hi, lo], axis=1), sums, preferred_element_type=F32)
        return log_beta, cs[:, :width], cs[:, width:]

    mask = lax.broadcasted_iota(jnp.int32, (tq, tq), 1) < lax.broadcasted_iota(jnp.int32, (tq, tq), 0)
    own_sums = _suffix_and_total(tq, tk)
    carries, accs = [], []
    for hh in heads:
        log_beta, later, total = block_terms(scores(hh, ks_ref[0, 0, hh]), mask, own_sums, tq)
        w = jnp.where(mask, jnp.exp(log_beta + later), 0.0)
        accs.append(jnp.dot(w.astype(BF16), vs_ref[0, 0, hh].astype(BF16), preferred_element_type=F32))
        carries.append(total)
    past_sums = _suffix_and_total(tk, tk)

    def live(carries):
        return jnp.max(functools.reduce(jnp.maximum, carries)) > UNDERFLOW_LOG

    def block_pair(it, carries, accs):
        ja_raw = jd - 1 - 2 * it
        has_a, has_b = ja_raw >= 0, ja_raw >= 1
        ja, jb = jnp.maximum(ja_raw, 0), jnp.maximum(ja_raw - 1, 0)
        new_carries, new_accs = [], []
        for hh in heads:
            lb_a, later_a, total_a = block_terms(scores(hh, past_keys(hh, ja)), None, past_sums, tk)
            lb_b, later_b, total_b = block_terms(scores(hh, past_keys(hh, jb)), None, past_sums, tk)
            carry_b = carries[hh] + jnp.where(has_a, total_a, 0.0)
            w_a = jnp.exp(lb_a + later_a + carries[hh]).astype(BF16)
            w_b = jnp.exp(lb_b + later_b + carry_b).astype(BF16)
            w_a = jnp.where(has_a, w_a, jnp.zeros_like(w_a))
            w_b = jnp.where(has_b, w_b, jnp.zeros_like(w_b))
            new_accs.append(accs[hh]
                            + jnp.dot(w_a, past_values(hh, ja), preferred_element_type=F32)
                            + jnp.dot(w_b, past_values(hh, jb), preferred_element_type=F32))
            new_carries.append(carry_b + jnp.where(has_b, total_b, 0.0))
        return tuple(new_carries), tuple(new_accs)

    def cond(state):
        it, _, _, go = state
        return (it < (jd + 1) // 2) & go

    def body(state):
        it, carries, accs, _ = state
        carries, accs = block_pair(it, carries, accs)
        return it + 1, carries, accs, live(carries)

    carries, accs = block_pair(0, tuple(carries), tuple(accs))
    _, _, accs, _ = lax.while_loop(cond, body, (jnp.int32(1), carries, accs, live(carries)))
    o_ref[0] = jnp.concatenate(accs, axis=1)


def stick_breaking(l, q, k, v, k_past, v_past, tq, tk):
    b, h, L, dh = q.shape
    hp = ATTN_HEADS_PER_STEP
    if k_past is None:
        assert tq % tk == 0
        k_past, v_past, n_past = k, v, None
    else:
        assert L == tq and k_past.shape[3] % tk == 0
        n_past = k_past.shape[3] // tk
    kern = functools.partial(_attn_kernel, tq=tq, tk=tk, n_past=n_past)
    q_spec = pl.BlockSpec((1, hp, tq, dh), lambda i, j, t, l_ref: (i, j, t, 0))
    own_spec = pl.BlockSpec((1, 1, hp, tq, dh), lambda i, j, t, l_ref: (l_ref[0], i, j, t, 0))
    past_spec = pl.BlockSpec((1, 1, hp, k_past.shape[3], dh), lambda i, j, t, l_ref: (l_ref[0], i, j, 0, 0))
    return pl.pallas_call(
        kern,
        grid_spec=pltpu.PrefetchScalarGridSpec(
            num_scalar_prefetch=1,
            grid=(b, h // hp, L // tq),
            in_specs=[q_spec, own_spec, own_spec, past_spec, past_spec],
            out_specs=pl.BlockSpec((1, tq, hp * dh), lambda i, j, t, l_ref: (i, t, j))),
        out_shape=jax.ShapeDtypeStruct((b, L, h * dh), F32),
        compiler_params=_cparams(3),
        name="stick_breaking",
    )(l, q, k, v, k_past, v_past)


def _lane_pack(cols, tm):
    lane = lax.broadcasted_iota(jnp.int32, (tm, LANES), 1)
    out = jnp.zeros((tm, LANES), cols[0].dtype)
    for kk, col in enumerate(cols):
        out = jnp.where(lane == kk, col, out)
    return out


def _outproj_kernel(l_ref, s5_ref, at_ref, x_ref, g1_ref, sc2_ref, sh2_ref, w_ref, lg_ref, lb_ref,
                    wr_ref, br_ref, x1_ref, h2_ref, idx_ref, gate_ref, rank_ref, cnt_ref, counts, *, tm):
    del l_ref
    first = (pl.program_id(0) == 0) & (pl.program_id(1) == 0)

    @pl.when(first)
    def _():
        counts[...] = jnp.zeros_like(counts)

    mix = (jnp.dot(s5_ref[0].astype(BF16), w_ref[0, :S5_WIDTH, :], preferred_element_type=F32)
           + jnp.dot(at_ref[0].astype(BF16), w_ref[0, S5_WIDTH:, :], preferred_element_type=F32))
    x1 = _layer_norm(DEEPNORM_ALPHA * x_ref[0] + (1.0 + g1_ref[0]) * mix, lg_ref[...], lb_ref[...])
    x1_ref[0] = x1
    h2 = x1 * (1.0 + sc2_ref[0]) + sh2_ref[0]
    for s in range(ROW_TILES):
        h2_ref[pl.ds(s, tm, stride=ROW_TILES), :] = h2[:, s * LANES:(s + 1) * LANES]

    h_hi = h2.astype(BF16)
    h_lo = (h2 - h_hi.astype(F32)).astype(BF16)
    by_hi = jnp.dot(h_hi, wr_ref[...], preferred_element_type=F32)
    logits = (by_hi[:, :LANES] + by_hi[:, LANES:]
              + jnp.dot(h_lo, wr_ref[:, :LANES], preferred_element_type=F32) + br_ref[...])
    lane = lax.broadcasted_iota(jnp.int32, (tm, LANES), 1)
    work = logits
    vals, idxs, hots = [], [], []
    for _ in range(TOP_K):
        m = jnp.max(work, axis=1, keepdims=True)
        sel = jnp.min(jnp.where(work == m, lane, LANES), axis=1, keepdims=True)
        hot = lane == sel
        work = jnp.where(hot, -jnp.inf, work)
        vals.append(m)
        idxs.append(sel)
        hots.append(hot)
    exps = [jnp.exp(vv - vals[0]) for vv in vals]
    denom = exps[0] + exps[1] + exps[2] + exps[3]
    gates = [e / denom for e in exps]

    hot_any = (hots[0] | hots[1] | hots[2] | hots[3]).astype(F32)
    r_idx = lax.broadcasted_iota(jnp.int32, (tm, tm), 0)
    c_idx = lax.broadcasted_iota(jnp.int32, (tm, tm), 1)
    earlier = jnp.where(c_idx < r_idx, 1.0, 0.0).astype(BF16)
    prefix = jnp.dot(earlier, hot_any.astype(BF16), preferred_element_type=F32) + counts[0:1, :]
    ranks = [jnp.sum(jnp.where(hot, prefix, 0.0), axis=1, keepdims=True).astype(jnp.int32) for hot in hots]
    counts[...] = counts[...] + jnp.sum(hot_any, axis=0, keepdims=True)

    idx_ref[...] = _lane_pack(idxs, tm)
    gate_ref[...] = _lane_pack(gates, tm)
    rank_ref[...] = _lane_pack(ranks, tm)
    cnt_ref[...] = counts[...]


def out_projection(l, s5_out, attn, x, g1, sc2, sh2, w_out_bf16, ln_g, ln_b, w_router_pad, b_router_pad, tm):
    b, L, _ = x.shape
    t = b * L
    nj = L // tm
    kern = functools.partial(_outproj_kernel, tm=tm)
    mod_spec = pl.BlockSpec((1, 1, D_MODEL), lambda i, j, l_ref: (i, 0, 0))
    vec_spec = pl.BlockSpec((1, D_MODEL), lambda i, j, l_ref: (0, 0))
    half_spec = pl.BlockSpec((1, tm, S5_WIDTH), lambda i, j, l_ref: (i, j, 0))
    full_spec = pl.BlockSpec((1, tm, D_MODEL), lambda i, j, l_ref: (i, j, 0))
    tok_spec = pl.BlockSpec((tm, LANES), lambda i, j, l_ref: (i * nj + j, 0))
    return pl.pallas_call(
        kern,
        grid_spec=pltpu.PrefetchScalarGridSpec(
            num_scalar_prefetch=1,
            grid=(b, nj),
            in_specs=[half_spec, half_spec, full_spec, mod_spec, mod_spec, mod_spec,
                      pl.BlockSpec((1, D_MODEL, D_MODEL), lambda i, j, l_ref: (l_ref[0], 0, 0)),
                      vec_spec, vec_spec,
                      pl.BlockSpec((D_MODEL, 2 * LANES), lambda i, j, l_ref: (0, 0)),
                      pl.BlockSpec((1, LANES), lambda i, j, l_ref: (0, 0))],
            out_specs=[full_spec,
                       pl.BlockSpec((tm * ROW_TILES, LANES), lambda i, j, l_ref: (i * nj + j, 0)),
                       tok_spec, tok_spec, tok_spec,
                       pl.BlockSpec((SUBLANES, LANES), lambda i, j, l_ref: (0, 0))],
            scratch_shapes=[pltpu.VMEM((SUBLANES, LANES), F32)]),
        out_shape=[jax.ShapeDtypeStruct((b, L, D_MODEL), F32),
                   jax.ShapeDtypeStruct((t * ROW_TILES, LANES), F32),
                   jax.ShapeDtypeStruct((t, LANES), jnp.int32),
                   jax.ShapeDtypeStruct((t, LANES), F32),
                   jax.ShapeDtypeStruct((t, LANES), jnp.int32),
                   jax.ShapeDtypeStruct((SUBLANES, LANES), F32)],
        compiler_params=_cparams(2),
        name="out_proj_router",
    )(l, s5_out, attn, x, g1, sc2, sh2, w_out_bf16, ln_g, ln_b, w_router_pad, b_router_pad)


def _row_copy(src, src_row, dst, dst_row, sem):
    return pltpu.make_async_copy(
        src.at[pl.ds(pl.multiple_of(src_row * ROW_TILES, ROW_TILES), ROW_TILES)],
        dst.at[pl.ds(pl.multiple_of(dst_row * ROW_TILES, ROW_TILES), ROW_TILES)], sem)


def _rows_wait(big_ref, n_rows, sem):
    span = big_ref.at[pl.ds(0, n_rows * ROW_TILES)]
    pltpu.make_async_copy(span, span, sem).wait()


def _dispatch_kernel(lo_ref, hi_ref, dest_ref, h2_ref, xs_ref, zero_tile, sem, fill_sem, *, tm, n_fill):
    n = tm * TOP_K

    @pl.when(pl.program_id(0) == 0)
    def _():
        zero_tile[...] = jnp.zeros_like(zero_tile)

        def fill_range(e, _):
            def fill_row(r, _):
                _row_copy(zero_tile, 0, xs_ref, r, fill_sem).start()
                return 0
            lax.fori_loop(lo_ref[e], hi_ref[e], fill_row, 0)
            return 0

        lax.fori_loop(0, N_EXPERTS + 1, fill_range, 0)
        _rows_wait(xs_ref, n_fill, fill_sem)

    def issue(t, _):
        for kk in range(TOP_K):
            _row_copy(h2_ref, t, xs_ref, dest_ref[t * TOP_K + kk], sem).start(priority=kk % 2)
        return 0

    lax.fori_loop(0, tm, issue, 0, unroll=2)
    _rows_wait(xs_ref, n, sem)


def moe_dispatch(fill_lo, fill_hi, dest_flat, h2_tiles, n_rows, tm):
    n_assign = dest_flat.shape[0]
    kern = functools.partial(_dispatch_kernel, tm=tm, n_fill=n_rows - n_assign)
    return pl.pallas_call(
        kern,
        grid_spec=pltpu.PrefetchScalarGridSpec(
            num_scalar_prefetch=2,
            grid=(n_assign // (tm * TOP_K),),
            in_specs=[pl.BlockSpec((tm * TOP_K,), lambda i, lo, hi: (i,), memory_space=pltpu.SMEM),
                      pl.BlockSpec((tm * ROW_TILES, LANES), lambda i, lo, hi: (i, 0))],
            out_specs=pl.BlockSpec(memory_space=pl.ANY),
            scratch_shapes=[pltpu.VMEM((ROW_TILES, LANES), F32),
                            pltpu.SemaphoreType.DMA(()), pltpu.SemaphoreType.DMA(())]),
        out_shape=jax.ShapeDtypeStruct((n_rows * ROW_TILES, LANES), F32),
        compiler_params=_cparams(1),
        name="moe_dispatch",
    )(fill_lo, fill_hi, dest_flat, h2_tiles)


def _expert_kernel(l_ref, be_ref, nv_ref, xs_ref, wgu_ref, bgu_ref, wd_ref, bd_ref, ys_ref, *, rb):
    del l_ref, be_ref
    nvalid = nv_ref[pl.program_id(0)]

    @pl.when(nvalid > 0)
    def _():
        x = jnp.concatenate([xs_ref[pl.ds(s, rb, stride=ROW_TILES), :] for s in range(ROW_TILES)], axis=1)
        gu = jnp.dot(x.astype(BF16), wgu_ref[0, 0], preferred_element_type=F32) + bgu_ref[0, 0]
        gate = jnp.minimum(gu[:, :D_FF], SWIGLU_LIMIT)
        up = jnp.clip(gu[:, D_FF:], -SWIGLU_LIMIT, SWIGLU_LIMIT)
        act = (up + 1.0) * gate * jax.nn.sigmoid(SWIGLU_ALPHA * gate)
        y = jnp.dot(act.astype(BF16), wd_ref[0, 0], preferred_element_type=F32) + bd_ref[0, 0]
        for s in range(ROW_TILES):
            ys_ref[pl.ds(s, rb, stride=ROW_TILES), :] = y[:, s * LANES:(s + 1) * LANES]

    @pl.when(nvalid <= 0)
    def _():
        ys_ref[...] = jnp.zeros_like(ys_ref)


def moe_experts(l, block_expert, block_valid, xs_tiles, wgu_bf16, b_gate_up, wd_bf16, b_down, rb):
    n_blocks = block_expert.shape[0]
    kern = functools.partial(_expert_kernel, rb=rb)
    row_spec = pl.BlockSpec((rb * ROW_TILES, LANES), lambda i, l_ref, be, nv: (i, 0))
    return pl.pallas_call(
        kern,
        grid_spec=pltpu.PrefetchScalarGridSpec(
            num_scalar_prefetch=3,
            grid=(n_blocks,),
            in_specs=[row_spec,
                      pl.BlockSpec((1, 1, D_MODEL, 2 * D_FF), lambda i, l_ref, be, nv: (l_ref[0], be[i], 0, 0)),
                      pl.BlockSpec((1, 1, 1, 2 * D_FF), lambda i, l_ref, be, nv: (l_ref[0], be[i], 0, 0)),
                      pl.BlockSpec((1, 1, D_FF, D_MODEL), lambda i, l_ref, be, nv: (l_ref[0], be[i], 0, 0)),
                      pl.BlockSpec((1, 1, 1, D_MODEL), lambda i, l_ref, be, nv: (l_ref[0], be[i], 0, 0))],
            out_specs=row_spec),
        out_shape=jax.ShapeDtypeStruct(xs_tiles.shape, F32),
        compiler_params=_cparams(1),
        name="moe_experts",
    )(l, block_expert, block_valid, xs_tiles, wgu_bf16,
      b_gate_up.reshape(DEPTH, N_EXPERTS, 1, 2 * D_FF), wd_bf16, b_down.reshape(DEPTH, N_EXPERTS, 1, D_MODEL))


COMBINE_CHUNK = 8 * SUBLANES


def _combine_kernel(dest_ref, next_ref, ys_ref, gate_ref, x1_ref, g2_ref, lg_ref, lb_ref, o_ref, buf, sems,
                    *, tm, n_steps):
    n = tm * TOP_K
    step = pl.program_id(0) * pl.num_programs(1) + pl.program_id(1)
    half = step % 2

    def issue(table, into, t):
        for kk in range(TOP_K):
            _row_copy(ys_ref, table[t * TOP_K + kk], buf, into * n + kk * tm + t,
                      sems.at[into]).start(priority=kk % 2)

    @pl.when(step == 0)
    def _():
        def first(t, _):
            issue(dest_ref, half, t)
            return 0
        lax.fori_loop(0, tm, first, 0, unroll=2)

    _rows_wait(buf, n, sems.at[half])

    base = half * n * ROW_TILES
    scale2 = 1.0 + g2_ref[0]

    cs = min(tm, COMBINE_CHUNK)

    def chunk(c, _):
        t0 = pl.multiple_of(c * cs, cs)
        for i in range(cs):
            issue(next_ref, 1 - half, t0 + i)
        gates = gate_ref[pl.ds(t0, cs), :]
        ff = jnp.zeros((cs, D_MODEL), F32)
        for kk in range(TOP_K):
            first_row = base + (kk * tm + t0) * ROW_TILES
            rows = jnp.concatenate(
                [buf[pl.ds(first_row + s, cs, stride=ROW_TILES), :] for s in range(ROW_TILES)], axis=1)
            ff = ff + rows * gates[:, kk:kk + 1]
        y = DEEPNORM_ALPHA * x1_ref[0, pl.ds(t0, cs), :] + scale2 * ff
        o_ref[0, pl.ds(t0, cs), :] = _layer_norm(y, lg_ref[...], lb_ref[...])
        return 0

    assert tm % cs == 0
    lax.fori_loop(0, tm // cs, chunk, 0)

    @pl.when(step == n_steps - 1)
    def _():
        _rows_wait(buf, n, sems.at[1 - half])


def moe_combine(dest_flat, ys_tiles, gates, x1, g2, ln_g, ln_b, tm):
    b, L, _ = x1.shape
    nj = L // tm
    n_steps = b * nj
    kern = functools.partial(_combine_kernel, tm=tm, n_steps=n_steps)
    full_spec = pl.BlockSpec((1, tm, D_MODEL), lambda i, j: (i, j, 0))
    vec_spec = pl.BlockSpec((1, D_MODEL), lambda i, j: (0, 0))
    return pl.pallas_call(
        kern,
        grid=(b, nj),
        in_specs=[pl.BlockSpec((tm * TOP_K,), lambda i, j: (i * nj + j,), memory_space=pltpu.SMEM),
                  pl.BlockSpec((tm * TOP_K,), lambda i, j: (jnp.minimum(i * nj + j + 1, n_steps - 1),),
                               memory_space=pltpu.SMEM),
                  pl.BlockSpec(memory_space=pl.ANY),
                  pl.BlockSpec((tm, LANES), lambda i, j: (i * nj + j, 0)),
                  full_spec,
                  pl.BlockSpec((1, 1, D_MODEL), lambda i, j: (i, 0, 0)),
                  vec_spec, vec_spec],
        out_specs=full_spec,
        out_shape=jax.ShapeDtypeStruct((b, L, D_MODEL), F32),
        scratch_shapes=[pltpu.VMEM((2 * tm * TOP_K * ROW_TILES, LANES), F32), pltpu.SemaphoreType.DMA((2,))],
        compiler_params=_cparams(2),
        name="moe_combine",
    )(dest_flat, dest_flat, ys_tiles, gates, x1, g2, ln_g, ln_b)


def _routing_tables(counts, top_idx, rank, rb, n_blocks):
    counts = counts.astype(jnp.int32)
    padded = (counts + rb - 1) // rb * rb
    pend = jnp.cumsum(padded)
    pstart = pend - padded
    dest = (pstart[top_idx] + rank).reshape(-1)
    row0 = jnp.arange(n_blocks, dtype=jnp.int32) * rb
    ends_passed = jnp.sum((pend[None, :] <= row0[:, None]).astype(jnp.int32), axis=1)
    block_expert = jnp.minimum(ends_passed, N_EXPERTS - 1).astype(jnp.int32)
    block_valid = jnp.clip(counts[block_expert] - (row0 - pstart[block_expert]), 0, rb).astype(jnp.int32)
    tail = jnp.full((1,), n_blocks * rb, jnp.int32)
    fill_lo = jnp.concatenate([pstart + counts, pend[-1:]]).astype(jnp.int32)
    fill_hi = jnp.concatenate([pend, tail]).astype(jnp.int32)
    return dest.astype(jnp.int32), block_expert, block_valid, fill_lo, fill_hi


def _tiles(b, L):
    t = b * L
    return dict(
        tm_proj=min(L, 512),
        lc=min(L, 128),
        tq=min(L, 128),
        tm_out=min(L, 512),
        tm_moe=min(L, 256),
        tm_disp=min(t, 512),
        rb=512 if t >= 8192 else 64,
    )


def _run_layer(l, x, mod, k_all, v_all, k_past, v_past, s0_re, s0_im, prm):
    b, L, _ = x.shape
    t = b * L
    tl = _tiles(b, L)
    sh1, sc1, g1, sh2, sc2, g2 = [m[:, None, :] for m in jnp.split(mod, 6, axis=-1)]

    u, q, k, v = in_projection(l, x, sc1, sh1, prm['w_in'], k_all, v_all, tl['tm_proj'])
    s5_out, s_re, s_im = s5_mixer(l, u, s0_re, s0_im, prm['wb'], prm['ab_re'], prm['ab_im'], prm['wc'],
                                  prm['s5_d'], prm['w_glu'], prm['b_glu'], tl['lc'])
    attn = stick_breaking(l, q, k, v, k_past, v_past, tl['tq'], ATTN_KEY_BLOCK)

    x1, h2_tiles, top_idx, gates, rank, counts = out_projection(
        l, s5_out, attn, x, g1, sc2, sh2, prm['w_out'], prm['ln1_g'], prm['ln1_b'],
        prm['w_router'], prm['b_router'], tl['tm_out'])

    rb = tl['rb']
    n_assign = t * TOP_K
    n_blocks = (n_assign + N_EXPERTS * (rb - 1) + rb - 1) // rb
    dest, block_expert, block_valid, fill_lo, fill_hi = _routing_tables(
        counts[0, :N_EXPERTS], top_idx[:, :TOP_K], rank[:, :TOP_K], rb, n_blocks)
    xs_tiles = moe_dispatch(fill_lo, fill_hi, dest, h2_tiles, n_blocks * rb, tl['tm_disp'])
    ys_tiles = moe_experts(l, block_expert, block_valid, xs_tiles, prm['w_gate_up'], prm['b_gate_up'],
                           prm['w_down'], prm['b_down'], rb)
    x2 = moe_combine(dest, ys_tiles, gates, x1, g2, prm['ln2_g'], prm['ln2_b'], tl['tm_moe'])

    hshape = (b, S5_GROUPS, S5_STATE)
    return x2, k, v, s_re.reshape(hshape), s_im.reshape(hshape)


def kernel(x_prompt, x_sample, cache_k, cache_v, state_s5_re, state_s5_im, c_prompt, c_sample, ln_in_g, ln_in_b, w_ada, b_ada, w_in, s5_a_re, s5_a_im, s5_log_dt, s5_b_re, s5_b_im, s5_c_re, s5_c_im, s5_d, s5_w_glu, s5_b_glu, w_out, ln1_g, ln1_b, w_router, b_router, w_gate_up, b_gate_up, w_down, b_down, ln2_g, ln2_b):
    bp, lp, _ = x_prompt.shape
    bs, ls, _ = x_sample.shape

    mod_all = adaln_all(jnp.concatenate([c_prompt, c_sample], axis=0), w_ada, b_ada)
    xp = ln_rows(x_prompt.reshape(bp * lp, D_MODEL), ln_in_g, ln_in_b, min(512, bp * lp)).reshape(bp, lp, D_MODEL)
    xs = ln_rows(x_sample.reshape(bs * ls, D_MODEL), ln_in_g, ln_in_b, min(512, bs * ls)).reshape(bs, ls, D_MODEL)

    w_in_bf = w_in.astype(BF16)
    w_out_bf = w_out.astype(BF16)
    w_glu_bf = s5_w_glu.astype(BF16)
    wgu_bf = w_gate_up.astype(BF16)
    wd_bf = w_down.astype(BF16)
    w_router_pad = jnp.pad(w_router.astype(F32), ((0, 0), (0, 0), (0, LANES - N_EXPERTS)))
    w_router_hi = w_router_pad.astype(BF16)
    w_router_lo = (w_router_pad - w_router_hi.astype(F32)).astype(BF16)
    w_router_pad = jnp.concatenate([w_router_hi, w_router_lo], axis=2)
    b_router_pad = jnp.pad(b_router.astype(F32), ((0, 0), (0, LANES - N_EXPERTS)), constant_values=-1e30)
    zeros_p = jnp.zeros((bp, S5_NSTATE), F32)

    kp = jnp.zeros((DEPTH, bp, SB_HEADS, lp, SB_HEAD_DIM), F32)
    vp = jnp.zeros((DEPTH, bp, SB_HEADS, lp, SB_HEAD_DIM), F32)
    ksm = jnp.zeros((DEPTH, bs, SB_HEADS, ls, SB_HEAD_DIM), F32)
    vsm = jnp.zeros((DEPTH, bs, SB_HEADS, ls, SB_HEAD_DIM), F32)
    srp, sip, srs, sis = [], [], [], []
    for li in range(DEPTH):
        l = jnp.full((1,), li, jnp.int32)
        ab_re, ab_im, wb, wc = s5_discretise(s5_a_re[li], s5_a_im[li], s5_log_dt[li], s5_b_re[li], s5_b_im[li],
                                             s5_c_re[li], s5_c_im[li])
        prm = dict(w_in=w_in_bf, w_out=w_out_bf, w_glu=w_glu_bf, w_gate_up=wgu_bf, w_down=wd_bf,
                   b_gate_up=b_gate_up, b_down=b_down,
                   ab_re=ab_re, ab_im=ab_im, wb=wb, wc=wc,
                   s5_d=s5_d[li].reshape(1, S5_WIDTH), b_glu=s5_b_glu[li].reshape(1, S5_WIDTH),
                   ln1_g=ln1_g[li].reshape(1, D_MODEL), ln1_b=ln1_b[li].reshape(1, D_MODEL),
                   ln2_g=ln2_g[li].reshape(1, D_MODEL), ln2_b=ln2_b[li].reshape(1, D_MODEL),
                   w_router=w_router_pad[li], b_router=b_router_pad[li].reshape(1, LANES))
        xp, kp, vp, s_re, s_im = _run_layer(l, xp, mod_all[li, :bp], kp, vp, None, None, zeros_p, zeros_p, prm)
        srp.append(s_re); sip.append(s_im)
        xs, ksm, vsm, s_re, s_im = _run_layer(l, xs, mod_all[li, bp:], ksm, vsm, cache_k, cache_v,
                                              state_s5_re[li].reshape(bs, S5_NSTATE),
                                              state_s5_im[li].reshape(bs, S5_NSTATE), prm)
        srs.append(s_re); sis.append(s_im)
    return (xp, xs, kp, vp, jnp.stack(srp), jnp.stack(sip), ksm, vsm, jnp.stack(srs), jnp.stack(sis))
```

```python
import functools
import math

import jax
import jax.numpy as jnp
from jax import lax
from jax.experimental import pallas as pl
from jax.experimental.pallas import tpu as pltpu

F32 = jnp.float32
BF16 = jnp.bfloat16
HIGHEST = lax.Precision.HIGHEST

D_MODEL = 1024
DEPTH = 4
S5_WIDTH = D_MODEL // 2
S5_GROUP_CH = 16
S5_GROUPS = S5_WIDTH // S5_GROUP_CH
S5_STATE = 64
S5_NSTATE = S5_GROUPS * S5_STATE
SB_WIDTH = D_MODEL - S5_WIDTH
SB_HEAD_DIM = 64
SB_HEADS = SB_WIDTH // SB_HEAD_DIM
N_EXPERTS = 32
TOP_K = 4
D_FF = D_MODEL
SWIGLU_LIMIT = 7.0
SWIGLU_ALPHA = 1.702
LN_EPS = 1e-5
DEEPNORM_ALPHA = (2 * DEPTH) ** 0.25

LANES = 128
SUBLANES = 8
ROW_TILES = D_MODEL // LANES
VMEM_LIMIT = 56 * 1024 * 1024


def _cparams(n_axes):
    return pltpu.CompilerParams(dimension_semantics=("arbitrary",) * n_axes,
                                vmem_limit_bytes=VMEM_LIMIT)


def _layer_norm(x, g, b):
    mu = jnp.mean(x, axis=-1, keepdims=True)
    xc = x - mu
    var = jnp.mean(xc * xc, axis=-1, keepdims=True)
    return xc * lax.rsqrt(var + LN_EPS) * g + b


def _ada_kernel(c_ref, w_ref, b_ref, o_ref):
    c = c_ref[...]
    s = c * jax.nn.sigmoid(c)
    o_ref[0] = jnp.dot(s, w_ref[0], precision=HIGHEST, preferred_element_type=F32) + b_ref[0]


def adaln_all(c_all, w_ada, b_ada):
    bc = c_all.shape[0]
    return pl.pallas_call(
        _ada_kernel,
        grid=(DEPTH, 6),
        in_specs=[pl.BlockSpec((bc, D_MODEL), lambda l, j: (0, 0)),
                  pl.BlockSpec((1, D_MODEL, D_MODEL), lambda l, j: (l, 0, j)),
                  pl.BlockSpec((1, 1, D_MODEL), lambda l, j: (l, 0, j))],
        out_specs=pl.BlockSpec((1, bc, D_MODEL), lambda l, j: (l, 0, j)),
        out_shape=jax.ShapeDtypeStruct((DEPTH, bc, 6 * D_MODEL), F32),
        compiler_params=_cparams(2),
        name="adaln",
    )(c_all, w_ada, b_ada.reshape(DEPTH, 1, 6 * D_MODEL))


def _ln_kernel(x_ref, g_ref, b_ref, o_ref):
    o_ref[...] = _layer_norm(x_ref[...], g_ref[...], b_ref[...])


def ln_rows(x2d, g, b, tm):
    t = x2d.shape[0]
    return pl.pallas_call(
        _ln_kernel,
        grid=(t // tm,),
        in_specs=[pl.BlockSpec((tm, D_MODEL), lambda i: (i, 0)),
                  pl.BlockSpec((1, D_MODEL), lambda i: (0, 0)),
                  pl.BlockSpec((1, D_MODEL), lambda i: (0, 0))],
        out_specs=pl.BlockSpec((tm, D_MODEL), lambda i: (i, 0)),
        out_shape=jax.ShapeDtypeStruct((t, D_MODEL), F32),
        compiler_params=_cparams(1),
        name="ln_in",
    )(x2d, g.reshape(1, D_MODEL), b.reshape(1, D_MODEL))


def _inproj_kernel(l_ref, x_ref, sc_ref, sh_ref, w_ref, k_all_ref, v_all_ref, u_ref, q_ref, k_ref, v_ref):
    del l_ref, k_all_ref, v_all_ref
    h = x_ref[0] * (1.0 + sc_ref[0]) + sh_ref[0]
    proj = jnp.dot(h.astype(BF16), w_ref[0], preferred_element_type=F32)
    u_ref[0] = proj[:, :S5_WIDTH]
    for hh in range(SB_HEADS):
        lo = S5_WIDTH + hh * SB_HEAD_DIM
        q_ref[0, hh] = proj[:, lo:lo + SB_HEAD_DIM]
        k_ref[0, 0, hh] = proj[:, lo + SB_WIDTH:lo + SB_WIDTH + SB_HEAD_DIM]
        v_ref[0, 0, hh] = proj[:, lo + 2 * SB_WIDTH:lo + 2 * SB_WIDTH + SB_HEAD_DIM]


def in_projection(l, x, sc1, sh1, w_in_bf16, k_all, v_all, tm):
    b, L, _ = x.shape
    n_proj = S5_WIDTH + 3 * SB_WIDTH
    mod_spec = pl.BlockSpec((1, 1, D_MODEL), lambda i, j, l_ref: (i, 0, 0))
    any_spec = pl.BlockSpec(memory_space=pl.ANY)
    kv_spec = pl.BlockSpec((1, 1, SB_HEADS, tm, SB_HEAD_DIM), lambda i, j, l_ref: (l_ref[0], i, 0, j, 0))
    kv_shape = jax.ShapeDtypeStruct(k_all.shape, F32)
    return pl.pallas_call(
        _inproj_kernel,
        grid_spec=pltpu.PrefetchScalarGridSpec(
            num_scalar_prefetch=1,
            grid=(b, L // tm),
            in_specs=[pl.BlockSpec((1, tm, D_MODEL), lambda i, j, l_ref: (i, j, 0)),
                      mod_spec, mod_spec,
                      pl.BlockSpec((1, D_MODEL, n_proj), lambda i, j, l_ref: (l_ref[0], 0, 0)),
                      any_spec, any_spec],
            out_specs=[pl.BlockSpec((1, tm, S5_WIDTH), lambda i, j, l_ref: (i, j, 0)),
                       pl.BlockSpec((1, SB_HEADS, tm, SB_HEAD_DIM), lambda i, j, l_ref: (i, 0, j, 0)),
                       kv_spec, kv_spec]),
        out_shape=[jax.ShapeDtypeStruct((b, L, S5_WIDTH), F32),
                   jax.ShapeDtypeStruct((b, SB_HEADS, L, SB_HEAD_DIM), F32), kv_shape, kv_shape],
        input_output_aliases={5: 2, 6: 3},
        compiler_params=_cparams(2),
        name="in_proj",
    )(l, x, sc1, sh1, w_in_bf16, k_all, v_all)


S5_BATCH = SUBLANES
S5_SLABS = S5_NSTATE // LANES
S5_CHUNKS = S5_WIDTH // LANES
S5_CHUNK_STATES = S5_NSTATE // S5_CHUNKS
S5_CHUNK_SLABS = S5_CHUNK_STATES // LANES


def _s5_kernel(l_ref, u_ref, x0r_ref, x0i_ref, wb_ref, ar_ref, ai_ref, wc_ref, d_ref, wg_ref, bg_ref,
               o_ref, sr_ref, si_ref, upad, slab, st_re, st_im, *, lc, pitch):
    del l_ref
    ci = pl.program_id(1)

    @pl.when(ci == 0)
    def _():
        upad[...] = jnp.zeros_like(upad)
        st_re[...] = x0r_ref[...]
        st_im[...] = x0i_ref[...]

    for b in range(S5_BATCH):
        upad[b * pitch:b * pitch + lc, :] = u_ref[b]

    for c in range(S5_CHUNKS):
        r = jnp.dot(upad[:, c * LANES:(c + 1) * LANES].astype(BF16), wb_ref[c],
                    preferred_element_type=F32)
        for j in range(S5_CHUNK_SLABS):
            slab[c * S5_CHUNK_SLABS + j] = r[:, j * LANES:(j + 1) * LANES]
            slab[S5_SLABS + c * S5_CHUNK_SLABS + j] = r[:, S5_CHUNK_STATES + j * LANES:
                                                          S5_CHUNK_STATES + (j + 1) * LANES]

    for c in range(S5_CHUNKS):
        lane0 = c * S5_CHUNK_STATES
        ar = [jnp.broadcast_to(ar_ref[:, lane0 + j * LANES:lane0 + (j + 1) * LANES], (S5_BATCH, LANES))
              for j in range(S5_CHUNK_SLABS)]
        ai = [jnp.broadcast_to(ai_ref[:, lane0 + j * LANES:lane0 + (j + 1) * LANES], (S5_BATCH, LANES))
              for j in range(S5_CHUNK_SLABS)]
        xr0 = tuple(st_re[:, lane0 + j * LANES:lane0 + (j + 1) * LANES] for j in range(S5_CHUNK_SLABS))
        xi0 = tuple(st_im[:, lane0 + j * LANES:lane0 + (j + 1) * LANES] for j in range(S5_CHUNK_SLABS))

        def step(t, carry, c=c, ar=ar, ai=ai):
            xr, xi = carry
            nxr, nxi = [], []
            for j in range(S5_CHUNK_SLABS):
                s_re = c * S5_CHUNK_SLABS + j
                s_im = S5_SLABS + s_re
                rows_t = pl.ds(t, S5_BATCH, stride=pitch)
                nr = ar[j] * xr[j] - ai[j] * xi[j] + slab[s_re, rows_t, :]
                ni = ar[j] * xi[j] + ai[j] * xr[j] + slab[s_im, rows_t, :]
                slab[s_re, rows_t, :] = nr
                slab[s_im, rows_t, :] = ni
                nxr.append(nr)
                nxi.append(ni)
            return tuple(nxr), tuple(nxi)

        xr, xi = lax.fori_loop(0, lc, step, (xr0, xi0), unroll=2)
        for j in range(S5_CHUNK_SLABS):
            st_re[:, lane0 + j * LANES:lane0 + (j + 1) * LANES] = xr[j]
            st_im[:, lane0 + j * LANES:lane0 + (j + 1) * LANES] = xi[j]

    sr_ref[...] = st_re[...]
    si_ref[...] = st_im[...]

    ys = []
    for c in range(S5_CHUNKS):
        xs = jnp.concatenate(
            [slab[c * S5_CHUNK_SLABS + j] for j in range(S5_CHUNK_SLABS)]
            + [slab[S5_SLABS + c * S5_CHUNK_SLABS + j] for j in range(S5_CHUNK_SLABS)], axis=1)
        ys.append(jnp.dot(xs.astype(BF16), wc_ref[c], preferred_element_type=F32))
    y = jnp.concatenate(ys, axis=1) + d_ref[...] * upad[...]
    g = jax.nn.gelu(y)
    gate = jnp.dot(g.astype(BF16), wg_ref[0], preferred_element_type=F32) + bg_ref[...]
    out = g * jax.nn.sigmoid(gate)
    for b in range(S5_BATCH):
        o_ref[b] = out[b * pitch:b * pitch + lc, :]


def s5_mixer(l, u, x0_re, x0_im, wb, ab_re, ab_im, wc, d, w_glu_bf16, b_glu, lc):
    b, L, _ = u.shape
    pitch = lc + SUBLANES
    rows = S5_BATCH * pitch
    kern = functools.partial(_s5_kernel, lc=lc, pitch=pitch)
    const2 = lambda i, j, l_ref: (0, 0)
    const3 = lambda i, j, l_ref: (0, 0, 0)
    state_spec = pl.BlockSpec((S5_BATCH, S5_NSTATE), lambda i, j, l_ref: (i, 0))
    return pl.pallas_call(
        kern,
        grid_spec=pltpu.PrefetchScalarGridSpec(
            num_scalar_prefetch=1,
            grid=(b // S5_BATCH, L // lc),
            in_specs=[pl.BlockSpec((S5_BATCH, lc, S5_WIDTH), lambda i, j, l_ref: (i, j, 0)),
                      state_spec, state_spec,
                      pl.BlockSpec((S5_CHUNKS, LANES, 2 * S5_CHUNK_STATES), const3),
                      pl.BlockSpec((1, S5_NSTATE), const2),
                      pl.BlockSpec((1, S5_NSTATE), const2),
                      pl.BlockSpec((S5_CHUNKS, 2 * S5_CHUNK_STATES, LANES), const3),
                      pl.BlockSpec((1, S5_WIDTH), const2),
                      pl.BlockSpec((1, S5_WIDTH, S5_WIDTH), lambda i, j, l_ref: (l_ref[0], 0, 0)),
                      pl.BlockSpec((1, S5_WIDTH), const2)],
            out_specs=[pl.BlockSpec((S5_BATCH, lc, S5_WIDTH), lambda i, j, l_ref: (i, j, 0)),
                       state_spec, state_spec],
            scratch_shapes=[pltpu.VMEM((rows, S5_WIDTH), F32),
                            pltpu.VMEM((2 * S5_SLABS, rows, LANES), F32),
                            pltpu.VMEM((S5_BATCH, S5_NSTATE), F32),
                            pltpu.VMEM((S5_BATCH, S5_NSTATE), F32)]),
        out_shape=[jax.ShapeDtypeStruct((b, L, S5_WIDTH), F32),
                   jax.ShapeDtypeStruct((b, S5_NSTATE), F32),
                   jax.ShapeDtypeStruct((b, S5_NSTATE), F32)],
        compiler_params=_cparams(2),
        name="s5_mixer",
    )(l, u, x0_re, x0_im, wb, ab_re, ab_im, wc, d, w_glu_bf16, b_glu)


def s5_discretise(a_re, a_im, log_dt, b_re, b_im, c_re, c_im):
    a_re, a_im = a_re.astype(F32), a_im.astype(F32)
    dt = jnp.exp(log_dt.astype(F32))[:, None]
    mag = jnp.exp(a_re * dt)
    ab_re = mag * jnp.cos(a_im * dt)
    ab_im = mag * jnp.sin(a_im * dt)
    den = a_re * a_re + a_im * a_im
    nr, ni = ab_re - 1.0, ab_im
    f_re = (nr * a_re + ni * a_im) / den
    f_im = (ni * a_re - nr * a_im) / den
    b_re, b_im = b_re.astype(F32), b_im.astype(F32)
    bb_re = f_re[..., None] * b_re - f_im[..., None] * b_im
    bb_im = f_re[..., None] * b_im + f_im[..., None] * b_re
    gpc = S5_GROUPS // S5_CHUNKS
    eye = jnp.eye(gpc, dtype=F32)

    def blockdiag_in(bb):
        bb = bb.reshape(S5_CHUNKS, gpc, S5_STATE, S5_GROUP_CH)
        m = jnp.einsum('cgpn,gh->cgnhp', bb, eye)
        return m.reshape(S5_CHUNKS, gpc * S5_GROUP_CH, gpc * S5_STATE)

    def blockdiag_out(cc):
        cc = cc.reshape(S5_CHUNKS, gpc, S5_GROUP_CH, S5_STATE)
        m = jnp.einsum('cgnp,gh->cgphn', cc, eye)
        return m.reshape(S5_CHUNKS, gpc * S5_STATE, gpc * S5_GROUP_CH)

    wb = jnp.concatenate([blockdiag_in(bb_re), blockdiag_in(bb_im)], axis=2).astype(BF16)
    wc = jnp.concatenate([blockdiag_out(c_re.astype(F32)), -blockdiag_out(c_im.astype(F32))],
                         axis=1).astype(BF16)
    return ab_re.reshape(1, S5_NSTATE), ab_im.reshape(1, S5_NSTATE), wb, wc


ATTN_HEADS_PER_STEP = 8
ATTN_KEY_BLOCK = LANES
UNDERFLOW_LOG = -104.0


def _suffix_and_total(width, total_lanes):
    shape = (2 * width, width + total_lanes)
    s_idx = lax.broadcasted_iota(jnp.int32, shape, 0) % width
    j_idx = lax.broadcasted_iota(jnp.int32, shape, 1)
    return jnp.where((s_idx > j_idx) | (j_idx >= width), 1.0, 0.0).astype(BF16)


def _attn_kernel(l_ref, q_ref, ks_ref, vs_ref, kp_ref, vp_ref, o_ref, *, tq, tk, n_past):
    del l_ref
    jd = pl.program_id(2) * (tq // tk) if n_past is None else n_past
    scale = SB_HEAD_DIM ** -0.5
    heads = range(ATTN_HEADS_PER_STEP)
    qs = [(q_ref[0, hh] * scale).astype(BF16) for hh in heads]

    def scores(hh, kb):
        return lax.dot_general(qs[hh], kb.astype(BF16), (((1,), (1,)), ((), ())), preferred_element_type=F32)

    def past_keys(hh, j):
        return kp_ref[0, 0, hh, pl.ds(pl.multiple_of(j * tk, tk), tk), :]

    def past_values(hh, j):
        return vp_ref[0, 0, hh, pl.ds(pl.multiple_of(j * tk, tk), tk), :].astype(BF16)

    def block_terms(z, mask, sums, width):
        sp = jnp.log(1.0 + jnp.exp(-jnp.abs(z)))
        log_not = -(jnp.maximum(z, 0.0) + sp)
        log_beta = jnp.minimum(z, 0.0) - sp
        if mask is not None:
            log_not = jnp.where(mask, log_not, 0.0)
        hi = log_not.astype(BF16)
        lo = (log_not - hi.astype(F32)).astype(BF16)
        cs = jnp.dot(jnp.concatenate([hi, lo], axis=1), sums, preferred_element_type=F32)
        return log_beta, cs[:, :width], cs[:, width:]

    mask = lax.broadcasted_iota(jnp.int32, (tq, tq), 1) < lax.broadcasted_iota(jnp.int32, (tq, tq), 0)
    own_sums = _suffix_and_total(tq, tk)
    carries, accs = [], []
    for hh in heads:
        log_beta, later, total = block_terms(scores(hh, ks_ref[0, 0, hh]), mask, own_sums, tq)
        w = jnp.where(mask, jnp.exp(log_beta + later), 0.0)
        accs.append(jnp.dot(w.astype(BF16), vs_ref[0, 0, hh].astype(BF16), preferred_element_type=F32))
        carries.append(total)
    past_sums = _suffix_and_total(tk, tk)

    def live(carries):
        return jnp.max(functools.reduce(jnp.maximum, carries)) > UNDERFLOW_LOG

    def block_pair(it, carries, accs):
        ja_raw = jd - 1 - 2 * it
        has_a, has_b = ja_raw >= 0, ja_raw >= 1
        ja, jb = jnp.maximum(ja_raw, 0), jnp.maximum(ja_raw - 1, 0)
        new_carries, new_accs = [], []
        for hh in heads:
            lb_a, later_a, total_a = block_terms(scores(hh, past_keys(hh, ja)), None, past_sums, tk)
            lb_b, later_b, total_b = block_terms(scores(hh, past_keys(hh, jb)), None, past_sums, tk)
            carry_b = carries[hh] + jnp.where(has_a, total_a, 0.0)
            w_a = jnp.exp(lb_a + later_a + carries[hh]).astype(BF16)
            w_b = jnp.exp(lb_b + later_b + carry_b).astype(BF16)
            w_a = jnp.where(has_a, w_a, jnp.zeros_like(w_a))
            w_b = jnp.where(has_b, w_b, jnp.zeros_like(w_b))
            new_accs.append(accs[hh]
                            + jnp.dot(w_a, past_values(hh, ja), preferred_element_type=F32)
                            + jnp.dot(w_b, past_values(hh, jb), preferred_element_type=F32))
            new_carries.append(carry_b + jnp.where(has_b, total_b, 0.0))
        return tuple(new_carries), tuple(new_accs)

    def cond(state):
        it, _, _, go = state
        return (it < (jd + 1) // 2) & go

    def body(state):
        it, carries, accs, _ = state
        carries, accs = block_pair(it, carries, accs)
        return it + 1, carries, accs, live(carries)

    carries, accs = block_pair(0, tuple(carries), tuple(accs))
    _, _, accs, _ = lax.while_loop(cond, body, (jnp.int32(1), carries, accs, live(carries)))
    o_ref[0] = jnp.concatenate(accs, axis=1)


def stick_breaking(l, q, k, v, k_past, v_past, tq, tk):
    b, h, L, dh = q.shape
    hp = ATTN_HEADS_PER_STEP
    if k_past is None:
        assert tq % tk == 0
        k_past, v_past, n_past = k, v, None
    else:
        assert L == tq and k_past.shape[3] % tk == 0
        n_past = k_past.shape[3] // tk
    kern = functools.partial(_attn_kernel, tq=tq, tk=tk, n_past=n_past)
    q_spec = pl.BlockSpec((1, hp, tq, dh), lambda i, j, t, l_ref: (i, j, t, 0))
    own_spec = pl.BlockSpec((1, 1, hp, tq, dh), lambda i, j, t, l_ref: (l_ref[0], i, j, t, 0))
    past_spec = pl.BlockSpec((1, 1, hp, k_past.shape[3], dh), lambda i, j, t, l_ref: (l_ref[0], i, j, 0, 0))
    return pl.pallas_call(
        kern,
        grid_spec=pltpu.PrefetchScalarGridSpec(
            num_scalar_prefetch=1,
            grid=(b, h // hp, L // tq),
            in_specs=[q_spec, own_spec, own_spec, past_spec, past_spec],
            out_specs=pl.BlockSpec((1, tq, hp * dh), lambda i, j, t, l_ref: (i, t, j))),
        out_shape=jax.ShapeDtypeStruct((b, L, h * dh), F32),
        compiler_params=_cparams(3),
        name="stick_breaking",
    )(l, q, k, v, k_past, v_past)


def _lane_pack(cols, tm):
    lane = lax.broadcasted_iota(jnp.int32, (tm, LANES), 1)
    out = jnp.zeros((tm, LANES), cols[0].dtype)
    for kk, col in enumerate(cols):
        out = jnp.where(lane == kk, col, out)
    return out


def _outproj_kernel(l_ref, s5_ref, at_ref, x_ref, g1_ref, sc2_ref, sh2_ref, w_ref, lg_ref, lb_ref,
                    wr_ref, br_ref, x1_ref, h2_ref, idx_ref, gate_ref, rank_ref, cnt_ref, counts, *, tm):
    del l_ref
    first = (pl.program_id(0) == 0) & (pl.program_id(1) == 0)

    @pl.when(first)
    def _():
        counts[...] = jnp.zeros_like(counts)

    mix = (jnp.dot(s5_ref[0].astype(BF16), w_ref[0, :S5_WIDTH, :], preferred_element_type=F32)
           + jnp.dot(at_ref[0].astype(BF16), w_ref[0, S5_WIDTH:, :], preferred_element_type=F32))
    x1 = _layer_norm(DEEPNORM_ALPHA * x_ref[0] + (1.0 + g1_ref[0]) * mix, lg_ref[...], lb_ref[...])
    x1_ref[0] = x1
    h2 = x1 * (1.0 + sc2_ref[0]) + sh2_ref[0]
    for s in range(ROW_TILES):
        h2_ref[pl.ds(s, tm, stride=ROW_TILES), :] = h2[:, s * LANES:(s + 1) * LANES]

    h_hi = h2.astype(BF16)
    h_lo = (h2 - h_hi.astype(F32)).astype(BF16)
    by_hi = jnp.dot(h_hi, wr_ref[...], preferred_element_type=F32)
    logits = (by_hi[:, :LANES] + by_hi[:, LANES:]
              + jnp.dot(h_lo, wr_ref[:, :LANES], preferred_element_type=F32) + br_ref[...])
    lane = lax.broadcasted_iota(jnp.int32, (tm, LANES), 1)
    work = logits
    vals, idxs, hots = [], [], []
    for _ in range(TOP_K):
        m = jnp.max(work, axis=1, keepdims=True)
        sel = jnp.min(jnp.where(work == m, lane, LANES), axis=1, keepdims=True)
        hot = lane == sel
        work = jnp.where(hot, -jnp.inf, work)
        vals.append(m)
        idxs.append(sel)
        hots.append(hot)
    exps = [jnp.exp(vv - vals[0]) for vv in vals]
    denom = exps[0] + exps[1] + exps[2] + exps[3]
    gates = [e / denom for e in exps]

    hot_any = (hots[0] | hots[1] | hots[2] | hots[3]).astype(F32)
    r_idx = lax.broadcasted_iota(jnp.int32, (tm, tm), 0)
    c_idx = lax.broadcasted_iota(jnp.int32, (tm, tm), 1)
    earlier = jnp.where(c_idx < r_idx, 1.0, 0.0).astype(BF16)
    prefix = jnp.dot(earlier, hot_any.astype(BF16), preferred_element_type=F32) + counts[0:1, :]
    ranks = [jnp.sum(jnp.where(hot, prefix, 0.0), axis=1, keepdims=True).astype(jnp.int32) for hot in hots]
    counts[...] = counts[...] + jnp.sum(hot_any, axis=0, keepdims=True)

    idx_ref[...] = _lane_pack(idxs, tm)
    gate_ref[...] = _lane_pack(gates, tm)
    rank_ref[...] = _lane_pack(ranks, tm)
    cnt_ref[...] = counts[...]


def out_projection(l, s5_out, attn, x, g1, sc2, sh2, w_out_bf16, ln_g, ln_b, w_router_pad, b_router_pad, tm):
    b, L, _ = x.shape
    t = b * L
    nj = L // tm
    kern = functools.partial(_outproj_kernel, tm=tm)
    mod_spec = pl.BlockSpec((1, 1, D_MODEL), lambda i, j, l_ref: (i, 0, 0))
    vec_spec = pl.BlockSpec((1, D_MODEL), lambda i, j, l_ref: (0, 0))
    half_spec = pl.BlockSpec((1, tm, S5_WIDTH), lambda i, j, l_ref: (i, j, 0))
    full_spec = pl.BlockSpec((1, tm, D_MODEL), lambda i, j, l_ref: (i, j, 0))
    tok_spec = pl.BlockSpec((tm, LANES), lambda i, j, l_ref: (i * nj + j, 0))
    return pl.pallas_call(
        kern,
        grid_spec=pltpu.PrefetchScalarGridSpec(
            num_scalar_prefetch=1,
            grid=(b, nj),
            in_specs=[half_spec, half_spec, full_spec, mod_spec, mod_spec, mod_spec,
                      pl.BlockSpec((1, D_MODEL, D_MODEL), lambda i, j, l_ref: (l_ref[0], 0, 0)),
                      vec_spec, vec_spec,
                      pl.BlockSpec((D_MODEL, 2 * LANES), lambda i, j, l_ref: (0, 0)),
                      pl.BlockSpec((1, LANES), lambda i, j, l_ref: (0, 0))],
            out_specs=[full_spec,
                       pl.BlockSpec((tm * ROW_TILES, LANES), lambda i, j, l_ref: (i * nj + j, 0)),
                       tok_spec, tok_spec, tok_spec,
                       pl.BlockSpec((SUBLANES, LANES), lambda i, j, l_ref: (0, 0))],
            scratch_shapes=[pltpu.VMEM((SUBLANES, LANES), F32)]),
        out_shape=[jax.ShapeDtypeStruct((b, L, D_MODEL), F32),
                   jax.ShapeDtypeStruct((t * ROW_TILES, LANES), F32),
                   jax.ShapeDtypeStruct((t, LANES), jnp.int32),
                   jax.ShapeDtypeStruct((t, LANES), F32),
                   jax.ShapeDtypeStruct((t, LANES), jnp.int32),
                   jax.ShapeDtypeStruct((SUBLANES, LANES), F32)],
        compiler_params=_cparams(2),
        name="out_proj_router",
    )(l, s5_out, attn, x, g1, sc2, sh2, w_out_bf16, ln_g, ln_b, w_router_pad, b_router_pad)


def _row_copy(src, src_row, dst, dst_row, sem):
    return pltpu.make_async_copy(
        src.at[pl.ds(pl.multiple_of(src_row * ROW_TILES, ROW_TILES), ROW_TILES)],
        dst.at[pl.ds(pl.multiple_of(dst_row * ROW_TILES, ROW_TILES), ROW_TILES)], sem)


def _rows_wait(big_ref, n_rows, sem):
    span = big_ref.at[pl.ds(0, n_rows * ROW_TILES)]
    pltpu.make_async_copy(span, span, sem).wait()


def _dispatch_kernel(lo_ref, hi_ref, dest_ref, h2_ref, xs_ref, zero_tile, sem, fill_sem, *, tm, n_fill):
    n = tm * TOP_K

    @pl.when(pl.program_id(0) == 0)
    def _():
        zero_tile[...] = jnp.zeros_like(zero_tile)

        def fill_range(e, _):
            def fill_row(r, _):
                _row_copy(zero_tile, 0, xs_ref, r, fill_sem).start()
                return 0
            lax.fori_loop(lo_ref[e], hi_ref[e], fill_row, 0)
            return 0

        lax.fori_loop(0, N_EXPERTS + 1, fill_range, 0)
        _rows_wait(xs_ref, n_fill, fill_sem)

    def issue(t, _):
        for kk in range(TOP_K):
            _row_copy(h2_ref, t, xs_ref, dest_ref[t * TOP_K + kk], sem).start(priority=kk % 2)
        return 0

    lax.fori_loop(0, tm, issue, 0, unroll=2)
    _rows_wait(xs_ref, n, sem)


def moe_dispatch(fill_lo, fill_hi, dest_flat, h2_tiles, n_rows, tm):
    n_assign = dest_flat.shape[0]
    kern = functools.partial(_dispatch_kernel, tm=tm, n_fill=n_rows - n_assign)
    return pl.pallas_call(
        kern,
        grid_spec=pltpu.PrefetchScalarGridSpec(
            num_scalar_prefetch=2,
            grid=(n_assign // (tm * TOP_K),),
            in_specs=[pl.BlockSpec((tm * TOP_K,), lambda i, lo, hi: (i,), memory_space=pltpu.SMEM),
                      pl.BlockSpec((tm * ROW_TILES, LANES), lambda i, lo, hi: (i, 0))],
            out_specs=pl.BlockSpec(memory_space=pl.ANY),
            scratch_shapes=[pltpu.VMEM((ROW_TILES, LANES), F32),
                            pltpu.SemaphoreType.DMA(()), pltpu.SemaphoreType.DMA(())]),
        out_shape=jax.ShapeDtypeStruct((n_rows * ROW_TILES, LANES), F32),
        compiler_params=_cparams(1),
        name="moe_dispatch",
    )(fill_lo, fill_hi, dest_flat, h2_tiles)


def _expert_kernel(l_ref, be_ref, nv_ref, xs_ref, wgu_ref, bgu_ref, wd_ref, bd_ref, ys_ref, *, rb):
    del l_ref, be_ref
    nvalid = nv_ref[pl.program_id(0)]

    @pl.when(nvalid > 0)
    def _():
        x = jnp.concatenate([xs_ref[pl.ds(s, rb, stride=ROW_TILES), :] for s in range(ROW_TILES)], axis=1)
        gu = jnp.dot(x.astype(BF16), wgu_ref[0, 0], preferred_element_type=F32) + bgu_ref[0, 0]
        gate = jnp.minimum(gu[:, :D_FF], SWIGLU_LIMIT)
        up = jnp.clip(gu[:, D_FF:], -SWIGLU_LIMIT, SWIGLU_LIMIT)
        act = (up + 1.0) * gate * jax.nn.sigmoid(SWIGLU_ALPHA * gate)
        y = jnp.dot(act.astype(BF16), wd_ref[0, 0], preferred_element_type=F32) + bd_ref[0, 0]
        for s in range(ROW_TILES):
            ys_ref[pl.ds(s, rb, stride=ROW_TILES), :] = y[:, s * LANES:(s + 1) * LANES]

    @pl.when(nvalid <= 0)
    def _():
        ys_ref[...] = jnp.zeros_like(ys_ref)


def moe_experts(l, block_expert, block_valid, xs_tiles, wgu_bf16, b_gate_up, wd_bf16, b_down, rb):
    n_blocks = block_expert.shape[0]
    kern = functools.partial(_expert_kernel, rb=rb)
    row_spec = pl.BlockSpec((rb * ROW_TILES, LANES), lambda i, l_ref, be, nv: (i, 0))
    return pl.pallas_call(
        kern,
        grid_spec=pltpu.PrefetchScalarGridSpec(
            num_scalar_prefetch=3,
            grid=(n_blocks,),
            in_specs=[row_spec,
                      pl.BlockSpec((1, 1, D_MODEL, 2 * D_FF), lambda i, l_ref, be, nv: (l_ref[0], be[i], 0, 0)),
                      pl.BlockSpec((1, 1, 1, 2 * D_FF), lambda i, l_ref, be, nv: (l_ref[0], be[i], 0, 0)),
                      pl.BlockSpec((1, 1, D_FF, D_MODEL), lambda i, l_ref, be, nv: (l_ref[0], be[i], 0, 0)),
                      pl.BlockSpec((1, 1, 1, D_MODEL), lambda i, l_ref, be, nv: (l_ref[0], be[i], 0, 0))],
            out_specs=row_spec),
        out_shape=jax.ShapeDtypeStruct(xs_tiles.shape, F32),
        compiler_params=_cparams(1),
        name="moe_experts",
    )(l, block_expert, block_valid, xs_tiles, wgu_bf16,
      b_gate_up.reshape(DEPTH, N_EXPERTS, 1, 2 * D_FF), wd_bf16, b_down.reshape(DEPTH, N_EXPERTS, 1, D_MODEL))


COMBINE_CHUNK = 8 * SUBLANES


def _combine_kernel(dest_ref, next_ref, ys_ref, gate_ref, x1_ref, g2_ref, lg_ref, lb_ref, o_ref, buf, sems,
                    *, tm, n_steps):
    n = tm * TOP_K
    step = pl.program_id(0) * pl.num_programs(1) + pl.program_id(1)
    half = step % 2

    def issue(table, into, t):
        for kk in range(TOP_K):
            _row_copy(ys_ref, table[t * TOP_K + kk], buf, into * n + kk * tm + t,
                      sems.at[into]).start(priority=kk % 2)

    @pl.when(step == 0)
    def _():
        def first(t, _):
            issue(dest_ref, half, t)
            return 0
        lax.fori_loop(0, tm, first, 0, unroll=2)

    _rows_wait(buf, n, sems.at[half])

    base = half * n * ROW_TILES
    scale2 = 1.0 + g2_ref[0]

    cs = min(tm, COMBINE_CHUNK)

    def chunk(c, _):
        t0 = pl.multiple_of(c * cs, cs)
        for i in range(cs):
            issue(next_ref, 1 - half, t0 + i)
        gates = gate_ref[pl.ds(t0, cs), :]
        ff = jnp.zeros((cs, D_MODEL), F32)
        for kk in range(TOP_K):
            first_row = base + (kk * tm + t0) * ROW_TILES
            rows = jnp.concatenate(
                [buf[pl.ds(first_row + s, cs, stride=ROW_TILES), :] for s in range(ROW_TILES)], axis=1)
            ff = ff + rows * gates[:, kk:kk + 1]
        y = DEEPNORM_ALPHA * x1_ref[0, pl.ds(t0, cs), :] + scale2 * ff
        o_ref[0, pl.ds(t0, cs), :] = _layer_norm(y, lg_ref[...], lb_ref[...])
        return 0

    assert tm % cs == 0
    lax.fori_loop(0, tm // cs, chunk, 0)

    @pl.when(step == n_steps - 1)
    def _():
        _rows_wait(buf, n, sems.at[1 - half])


def moe_combine(dest_flat, ys_tiles, gates, x1, g2, ln_g, ln_b, tm):
    b, L, _ = x1.shape
    nj = L // tm
    n_steps = b * nj
    kern = functools.partial(_combine_kernel, tm=tm, n_steps=n_steps)
    full_spec = pl.BlockSpec((1, tm, D_MODEL), lambda i, j: (i, j, 0))
    vec_spec = pl.BlockSpec((1, D_MODEL), lambda i, j: (0, 0))
    return pl.pallas_call(
        kern,
        grid=(b, nj),
        in_specs=[pl.BlockSpec((tm * TOP_K,), lambda i, j: (i * nj + j,), memory_space=pltpu.SMEM),
                  pl.BlockSpec((tm * TOP_K,), lambda i, j: (jnp.minimum(i * nj + j + 1, n_steps - 1),),
                               memory_space=pltpu.SMEM),
                  pl.BlockSpec(memory_space=pl.ANY),
                  pl.BlockSpec((tm, LANES), lambda i, j: (i * nj + j, 0)),
                  full_spec,
                  pl.BlockSpec((1, 1, D_MODEL), lambda i, j: (i, 0, 0)),
                  vec_spec, vec_spec],
        out_specs=full_spec,
        out_shape=jax.ShapeDtypeStruct((b, L, D_MODEL), F32),
        scratch_shapes=[pltpu.VMEM((2 * tm * TOP_K * ROW_TILES, LANES), F32), pltpu.SemaphoreType.DMA((2,))],
        compiler_params=_cparams(2),
        name="moe_combine",
    )(dest_flat, dest_flat, ys_tiles, gates, x1, g2, ln_g, ln_b)


def _routing_tables(counts, top_idx, rank, rb, n_blocks):
    counts = counts.astype(jnp.int32)
    padded = (counts + rb - 1) // rb * rb
    pend = jnp.cumsum(padded)
    pstart = pend - padded
    dest = (pstart[top_idx] + rank).reshape(-1)
    row0 = jnp.arange(n_blocks, dtype=jnp.int32) * rb
    ends_passed = jnp.sum((pend[None, :] <= row0[:, None]).astype(jnp.int32), axis=1)
    block_expert = jnp.minimum(ends_passed, N_EXPERTS - 1).astype(jnp.int32)
    block_valid = jnp.clip(counts[block_expert] - (row0 - pstart[block_expert]), 0, rb).astype(jnp.int32)
    tail = jnp.full((1,), n_blocks * rb, jnp.int32)
    fill_lo = jnp.concatenate([pstart + counts, pend[-1:]]).astype(jnp.int32)
    fill_hi = jnp.concatenate([pend, tail]).astype(jnp.int32)
    return dest.astype(jnp.int32), block_expert, block_valid, fill_lo, fill_hi


def _tiles(b, L):
    t = b * L
    return dict(
        tm_proj=min(L, 512),
        lc=min(L, 128),
        tq=min(L, 128),
        tm_out=min(L, 512),
        tm_moe=min(L, 256),
        tm_disp=min(t, 1024),
        rb=512 if t >= 8192 else 64,
    )


def _run_layer(l, x, mod, k_all, v_all, k_past, v_past, s0_re, s0_im, prm):
    b, L, _ = x.shape
    t = b * L
    tl = _tiles(b, L)
    sh1, sc1, g1, sh2, sc2, g2 = [m[:, None, :] for m in jnp.split(mod, 6, axis=-1)]

    u, q, k, v = in_projection(l, x, sc1, sh1, prm['w_in'], k_all, v_all, tl['tm_proj'])
    s5_out, s_re, s_im = s5_mixer(l, u, s0_re, s0_im, prm['wb'], prm['ab_re'], prm['ab_im'], prm['wc'],
                                  prm['s5_d'], prm['w_glu'], prm['b_glu'], tl['lc'])
    attn = stick_breaking(l, q, k, v, k_past, v_past, tl['tq'], ATTN_KEY_BLOCK)

    x1, h2_tiles, top_idx, gates, rank, counts = out_projection(
        l, s5_out, attn, x, g1, sc2, sh2, prm['w_out'], prm['ln1_g'], prm['ln1_b'],
        prm['w_router'], prm['b_router'], tl['tm_out'])

    rb = tl['rb']
    n_assign = t * TOP_K
    n_blocks = (n_assign + N_EXPERTS * (rb - 1) + rb - 1) // rb
    dest, block_expert, block_valid, fill_lo, fill_hi = _routing_tables(
        counts[0, :N_EXPERTS], top_idx[:, :TOP_K], rank[:, :TOP_K], rb, n_blocks)
    xs_tiles = moe_dispatch(fill_lo, fill_hi, dest, h2_tiles, n_blocks * rb, tl['tm_disp'])
    ys_tiles = moe_experts(l, block_expert, block_valid, xs_tiles, prm['w_gate_up'], prm['b_gate_up'],
                           prm['w_down'], prm['b_down'], rb)
    x2 = moe_combine(dest, ys_tiles, gates, x1, g2, prm['ln2_g'], prm['ln2_b'], tl['tm_moe'])

    hshape = (b, S5_GROUPS, S5_STATE)
    return x2, k, v, s_re.reshape(hshape), s_im.reshape(hshape)


def kernel(x_prompt, x_sample, cache_k, cache_v, state_s5_re, state_s5_im, c_prompt, c_sample, ln_in_g, ln_in_b, w_ada, b_ada, w_in, s5_a_re, s5_a_im, s5_log_dt, s5_b_re, s5_b_im, s5_c_re, s5_c_im, s5_d, s5_w_glu, s5_b_glu, w_out, ln1_g, ln1_b, w_router, b_router, w_gate_up, b_gate_up, w_down, b_down, ln2_g, ln2_b):
    bp, lp, _ = x_prompt.shape
    bs, ls, _ = x_sample.shape

    mod_all = adaln_all(jnp.concatenate([c_prompt, c_sample], axis=0), w_ada, b_ada)
    xp = ln_rows(x_prompt.reshape(bp * lp, D_MODEL), ln_in_g, ln_in_b, min(512, bp * lp)).reshape(bp, lp, D_MODEL)
    xs = ln_rows(x_sample.reshape(bs * ls, D_MODEL), ln_in_g, ln_in_b, min(512, bs * ls)).reshape(bs, ls, D_MODEL)

    w_in_bf = w_in.astype(BF16)
    w_out_bf = w_out.astype(BF16)
    w_glu_bf = s5_w_glu.astype(BF16)
    wgu_bf = w_gate_up.astype(BF16)
    wd_bf = w_down.astype(BF16)
    w_router_pad = jnp.pad(w_router.astype(F32), ((0, 0), (0, 0), (0, LANES - N_EXPERTS)))
    w_router_hi = w_router_pad.astype(BF16)
    w_router_lo = (w_router_pad - w_router_hi.astype(F32)).astype(BF16)
    w_router_pad = jnp.concatenate([w_router_hi, w_router_lo], axis=2)
    b_router_pad = jnp.pad(b_router.astype(F32), ((0, 0), (0, LANES - N_EXPERTS)), constant_values=-1e30)
    zeros_p = jnp.zeros((bp, S5_NSTATE), F32)

    kp = jnp.zeros((DEPTH, bp, SB_HEADS, lp, SB_HEAD_DIM), F32)
    vp = jnp.zeros((DEPTH, bp, SB_HEADS, lp, SB_HEAD_DIM), F32)
    ksm = jnp.zeros((DEPTH, bs, SB_HEADS, ls, SB_HEAD_DIM), F32)
    vsm = jnp.zeros((DEPTH, bs, SB_HEADS, ls, SB_HEAD_DIM), F32)
    srp, sip, srs, sis = [], [], [], []
    for li in range(DEPTH):
        l = jnp.full((1,), li, jnp.int32)
        ab_re, ab_im, wb, wc = s5_discretise(s5_a_re[li], s5_a_im[li], s5_log_dt[li], s5_b_re[li], s5_b_im[li],
                                             s5_c_re[li], s5_c_im[li])
        prm = dict(w_in=w_in_bf, w_out=w_out_bf, w_glu=w_glu_bf, w_gate_up=wgu_bf, w_down=wd_bf,
                   b_gate_up=b_gate_up, b_down=b_down,
                   ab_re=ab_re, ab_im=ab_im, wb=wb, wc=wc,
                   s5_d=s5_d[li].reshape(1, S5_WIDTH), b_glu=s5_b_glu[li].reshape(1, S5_WIDTH),
                   ln1_g=ln1_g[li].reshape(1, D_MODEL), ln1_b=ln1_b[li].reshape(1, D_MODEL),
                   ln2_g=ln2_g[li].reshape(1, D_MODEL), ln2_b=ln2_b[li].reshape(1, D_MODEL),
                   w_router=w_router_pad[li], b_router=b_router_pad[li].reshape(1, LANES))
        xp, kp, vp, s_re, s_im = _run_layer(l, xp, mod_all[li, :bp], kp, vp, None, None, zeros_p, zeros_p, prm)
        srp.append(s_re); sip.append(s_im)
        xs, ksm, vsm, s_re, s_im = _run_layer(l, xs, mod_all[li, bp:], ksm, vsm, cache_k, cache_v,
                                              state_s5_re[li].reshape(bs, S5_NSTATE),
                                              state_s5_im[li].reshape(bs, S5_NSTATE), prm)
        srs.append(s_re); sis.append(s_im)
    return (xp, xs, kp, vp, jnp.stack(srp), jnp.stack(sip), ksm, vsm, jnp.stack(srs), jnp.stack(sis))
```
